```python
import math
import jax, jax.numpy as jnp
from jax import lax
import numpy as np

D_MODEL = 1024
BATCH = 16
SEQ = 4096
DEPTH = 1

SB_WIDTH = D_MODEL // 2
SB_HEAD_DIM = 64
SB_HEADS = SB_WIDTH // SB_HEAD_DIM
ML_WIDTH = D_MODEL - SB_WIDTH
ML_HEADS = 4
ML_HEAD_DIM = ML_WIDTH // ML_HEADS
MIX_WIDTH = SB_WIDTH + ML_WIDTH
Q_BLOCK = 128
ML_CHUNK = 64
CONV_WIDTH = 4
EPS = 1e-6
IN_SPLITS = [SB_WIDTH] * 4 + [ML_WIDTH] * 5 + [ML_HEADS, ML_HEADS]
IN_WIDTH = sum(IN_SPLITS)

kernel_name = "hymba_stickbreaking_mlstm_adaln"


def _rmsnorm(t, gain):
    tf = t.astype(jnp.float32)
    tf = tf * lax.rsqrt(jnp.mean(tf * tf, axis=-1, keepdims=True) + EPS)
    return (tf * gain.astype(jnp.float32)).astype(t.dtype)


def _to_heads(t, n_heads):
    b, s, _ = t.shape
    return t.reshape(b, s, n_heads, -1).transpose(0, 2, 1, 3)


def _from_heads(t):
    b, h, s, d = t.shape
    return t.transpose(0, 2, 1, 3).reshape(b, s, h * d)


def _stick_breaking(q, k, v):
    seq = q.shape[2]
    scale = 1.0 / math.sqrt(q.shape[-1])
    outs = []
    for blk in range(seq // Q_BLOCK):
        t0 = blk * Q_BLOCK
        t1 = t0 + Q_BLOCK
        qb = q[:, :, t0:t1]
        kb = k[:, :, :t1]
        vb = v[:, :, :t1]
        z = jnp.einsum('bhqd,bhkd->bhqk', qb, kb).astype(jnp.float32) * scale
        qpos = t0 + jnp.arange(Q_BLOCK)[:, None]
        kpos = jnp.arange(t1)[None, :]
        strict = kpos < qpos
        log_1mb = jnp.where(strict, jax.nn.log_sigmoid(-z), 0.0)
        between = lax.cumsum(log_1mb, axis=3, reverse=True) - log_1mb
        a = jnp.where(strict, jnp.exp(jax.nn.log_sigmoid(z) + between), 0.0)
        outs.append(jnp.einsum('bhqk,bhkd->bhqd', a.astype(v.dtype), vb))
    return jnp.concatenate(outs, axis=2)


def _mlstm(q, k, v, i_pre, log_f):
    b, h, s, d = q.shape
    nc = s // ML_CHUNK
    f32 = jnp.float32
    qs = (q.astype(f32) * (1.0 / math.sqrt(d)))

    def chunks4(t):
        return t.reshape(b, h, nc, ML_CHUNK, -1).transpose(2, 0, 1, 3, 4)

    def chunks3(t):
        return t.reshape(b, h, nc, ML_CHUNK).transpose(2, 0, 1, 3)

    xs = (chunks4(qs), chunks4(k.astype(f32)), chunks4(v.astype(f32)),
          chunks3(i_pre.astype(f32)), chunks3(log_f.astype(f32)))
    causal = jnp.tril(jnp.ones((ML_CHUNK, ML_CHUNK), dtype=bool))

    def step(carry, xc):
        c_mat, n_vec, m_prev = carry
        qc, kc, vc, ic, fc = xc
        bcum = jnp.cumsum(fc, axis=-1)
        log_d = bcum[..., :, None] - bcum[..., None, :] + ic[..., None, :]
        log_d = jnp.where(causal, log_d, -jnp.inf)
        inter = bcum + m_prev[..., None]
        m_t = jnp.maximum(inter, jnp.max(log_d, axis=-1))
        d_mat = jnp.exp(log_d - m_t[..., None])
        g_inter = jnp.exp(inter - m_t)
        sc = jnp.einsum('bhld,bhsd->bhls', qc, kc) * d_mat
        num = (jnp.einsum('bhls,bhse->bhle', sc, vc)
               + g_inter[..., None] * jnp.einsum('bhld,bhde->bhle', qc, c_mat))
        den = jnp.sum(sc, axis=-1) + g_inter * jnp.einsum('bhld,bhd->bhl', qc, n_vec)
        h_out = num / jnp.maximum(jnp.abs(den), jnp.exp(-m_t))[..., None]
        b_last = bcum[..., -1]
        w_log = b_last[..., None] - bcum + ic
        m_new = jnp.maximum(b_last + m_prev, jnp.max(w_log, axis=-1))
        w = jnp.exp(w_log - m_new[..., None])
        g_c = jnp.exp(b_last + m_prev - m_new)
        c_new = g_c[..., None, None] * c_mat + jnp.einsum('bhs,bhsd,bhse->bhde', w, kc, vc)
        n_new = g_c[..., None] * n_vec + jnp.einsum('bhs,bhsd->bhd', w, kc)
        return (c_new, n_new, m_new), h_out

    init = (jnp.zeros((b, h, d, d), f32), jnp.zeros((b, h, d), f32), jnp.zeros((b, h), f32))
    _, hs = lax.scan(step, init, xs)
    return hs.transpose(1, 2, 0, 3, 4).reshape(b, h, s, d)


def _layer(x, c, w_ada, b_ada, norm_gain, w_in, b_gates, q_norm_gain, k_norm_gain,
           conv_w, conv_b, ml_norm_gain, w_out):
    bsz, seq, _ = x.shape
    mod = jax.nn.silu(c) @ w_ada + b_ada
    shift, scale, gate = jnp.split(mod, 3, axis=-1)
    hn = _rmsnorm(x, norm_gain) * (1.0 + scale[:, None, :]) + shift[:, None, :]
    u = hn @ w_in
    idx = list(np.cumsum(IN_SPLITS)[:-1])
    sb_q, sb_k, sb_v, sb_z, ml_q, ml_k, ml_v, ml_o, ml_z, ml_i, ml_f = jnp.split(u, idx, axis=-1)

    qh = _rmsnorm(_to_heads(sb_q, SB_HEADS), q_norm_gain)
    kh = _rmsnorm(_to_heads(sb_k, SB_HEADS), k_norm_gain)
    vh = _to_heads(sb_v, SB_HEADS)
    sb_out = _from_heads(_stick_breaking(qh, kh, vh))

    qk = jnp.concatenate([ml_q, ml_k], axis=-1)
    qk = lax.conv_general_dilated(
        qk, conv_w[:, None, :].astype(qk.dtype), window_strides=(1,),
        padding=[(CONV_WIDTH - 1, 0)], dimension_numbers=('NWC', 'WIO', 'NWC'),
        feature_group_count=2 * ML_WIDTH)
    qk = jax.nn.silu(qk + conv_b)
    mq, mk = jnp.split(qk, 2, axis=-1)
    gates = jnp.concatenate([ml_i, ml_f], axis=-1) + b_gates
    i_pre = gates[..., :ML_HEADS].transpose(0, 2, 1)
    log_f = jax.nn.log_sigmoid(gates[..., ML_HEADS:].astype(jnp.float32)).transpose(0, 2, 1)
    ml_h = _mlstm(_to_heads(mq, ML_HEADS), _to_heads(mk, ML_HEADS), _to_heads(ml_v, ML_HEADS),
                  i_pre, log_f)
    ml_h = _rmsnorm(ml_h, ml_norm_gain.reshape(ML_HEADS, 1, ML_HEAD_DIM)).astype(x.dtype)
    ml_out = jax.nn.sigmoid(ml_o) * _from_heads(ml_h)

    y = jnp.concatenate([sb_out * jax.nn.silu(sb_z), ml_out * jax.nn.silu(ml_z)], axis=-1)
    y = y @ w_out
    return x + gate[:, None, :] * y


def setup_inputs(seed: int = 0) -> dict:
    key = jax.random.key(seed)
    ks = jax.random.split(key, 16)
    f32 = jnp.float32
    nrm = lambda k, shp: jax.random.normal(k, shp, f32)
    x = nrm(ks[0], (BATCH, SEQ, D_MODEL))
    c = nrm(ks[1], (BATCH, D_MODEL))
    w_ada = nrm(ks[2], (DEPTH, D_MODEL, 3 * D_MODEL)) * (0.5 * D_MODEL ** -0.5)
    b_ada = nrm(ks[3], (DEPTH, 3 * D_MODEL)) * 0.02
    norm_gain = 1.0 + 0.02 * nrm(ks[4], (DEPTH, D_MODEL))
    w_in = nrm(ks[5], (DEPTH, D_MODEL, IN_WIDTH)) * D_MODEL ** -0.5
    i_bias = 0.02 * nrm(ks[6], (DEPTH, ML_HEADS))
    f_bias = jnp.linspace(3.0, 6.0, ML_HEADS, dtype=f32)[None, :] + 0.02 * nrm(ks[7], (DEPTH, ML_HEADS))
    b_gates = jnp.concatenate([i_bias, f_bias], axis=-1)
    q_norm_gain = 1.0 + 0.02 * nrm(ks[8], (DEPTH, SB_HEAD_DIM))
    k_norm_gain = 1.0 + 0.02 * nrm(ks[9], (DEPTH, SB_HEAD_DIM))
    conv_w = nrm(ks[10], (DEPTH, CONV_WIDTH, 2 * ML_WIDTH)) * CONV_WIDTH ** -0.5
    conv_b = 0.02 * nrm(ks[11], (DEPTH, 2 * ML_WIDTH))
    ml_norm_gain = 1.0 + 0.02 * nrm(ks[12], (DEPTH, ML_WIDTH))
    w_out = nrm(ks[13], (DEPTH, MIX_WIDTH, D_MODEL)) * MIX_WIDTH ** -0.5
    return {"x": x, "c": c, "w_ada": w_ada, "b_ada": b_ada, "norm_gain": norm_gain,
            "w_in": w_in, "b_gates": b_gates, "q_norm_gain": q_norm_gain,
            "k_norm_gain": k_norm_gain, "conv_w": conv_w, "conv_b": conv_b,
            "ml_norm_gain": ml_norm_gain, "w_out": w_out}


def reference(x, c, w_ada, b_ada, norm_gain, w_in, b_gates, q_norm_gain, k_norm_gain,
              conv_w, conv_b, ml_norm_gain, w_out):
    h = x
    for layer in range(DEPTH):
        h = _layer(h, c, w_ada[layer], b_ada[layer], norm_gain[layer], w_in[layer],
                   b_gates[layer], q_norm_gain[layer], k_norm_gain[layer], conv_w[layer],
                   conv_b[layer], ml_norm_gain[layer], w_out[layer])
    return h
```

```python
import functools
import math

import jax
import jax.numpy as jnp
from jax import lax
from jax.experimental import pallas as pl
from jax.experimental.pallas import tpu as pltpu

F32 = jnp.float32
BF16 = jnp.bfloat16

SB_HEAD_DIM = 64
ML_HEADS = 4
CONV_WIDTH = 4
EPS = 1e-6

LANES = 128
VMEM_LIMIT = 56 * 1024 * 1024

IN_TILE = 512
OUT_TILE = 512
SB_TQ = 128
SB_TK = 128
ML_CHUNK = 128
SB_LOG_STICK_FLOOR = -104.0

NT_DIMS = (((1,), (1,)), ((), ()))
TN_DIMS = (((0,), (0,)), ((), ()))


def _silu(u):
    return u / (1.0 + jnp.exp(-u))


def _sigmoid(u):
    return 1.0 / (1.0 + jnp.exp(-u))


def _log_sigmoid(u):
    return jnp.minimum(u, 0.0) - jnp.log1p(jnp.exp(-jnp.abs(u)))


def _mod_kernel(c_ref, w_ref, b_ref, o_ref):
    c = c_ref[...]
    o_ref[0] = jnp.dot(_silu(c), w_ref[...], preferred_element_type=F32) + b_ref[0]


def _mod(c, w_ada, b_ada):
    bsz, d = c.shape
    return pl.pallas_call(
        _mod_kernel,
        grid=(3,),
        in_specs=[pl.BlockSpec((bsz, d), lambda j: (0, 0)),
                  pl.BlockSpec((d, d), lambda j: (0, j)),
                  pl.BlockSpec((1, 1, d), lambda j: (j, 0, 0))],
        out_specs=pl.BlockSpec((1, bsz, d), lambda j: (j, 0, 0)),
        out_shape=jax.ShapeDtypeStruct((3, bsz, d), F32),
        compiler_params=pltpu.CompilerParams(dimension_semantics=("arbitrary",),
                                             vmem_limit_bytes=VMEM_LIMIT),
        name="adaln_mod",
    )(c, w_ada, b_ada.reshape(3, 1, d))


def _inproj_kernel(x_ref, mod_ref, gain_ref, w_ref, wg_ref, bg_ref, p_ref, qg_ref, kg_ref,
                   cw_ref, cb_ref,
                   qsb_ref, ksb_ref, vsb_ref, gsb_ref, qml_ref, kml_ref, vml_ref, gml_ref,
                   gates_ref, ext_ref, *, width, ml_scale):
    s = pl.program_id(1)
    ts = x_ref.shape[1]
    x = x_ref[0]
    ms = jnp.mean(x * x, axis=-1, keepdims=True)
    shift = mod_ref[0, 0]
    scale = mod_ref[1, 0]
    hn = (x * lax.rsqrt(ms + EPS)) * (gain_ref[...] * (1.0 + scale)) + shift
    hn = hn.astype(BF16)

    def proj(g):
        return jnp.dot(hn, w_ref[:, g * width:(g + 1) * width], preferred_element_type=F32)

    for g, gn_ref, out_ref, sc in ((0, qg_ref, qsb_ref, 1.0 / math.sqrt(SB_HEAD_DIM)),
                                   (1, kg_ref, ksb_ref, 1.0)):
        u = proj(g)
        msh = jnp.dot((u * u).astype(BF16), p_ref[...], preferred_element_type=F32)
        out_ref[0] = (u * lax.rsqrt(msh + EPS) * (gn_ref[...] * sc)).astype(BF16)
    vsb_ref[0] = proj(2).astype(BF16)
    gsb_ref[0] = _silu(proj(3)).astype(BF16)

    @pl.when(s == 0)
    def _():
        ext_ref[:, 0:8, :] = jnp.zeros((2, 8, width), F32)

    for idx, (g, out_ref, sc) in enumerate(((4, qml_ref, ml_scale), (5, kml_ref, 1.0))):
        ext_ref[idx, 8:, :] = proj(g)
        acc = cb_ref[:, idx * width:(idx + 1) * width]
        for j in range(CONV_WIDTH):
            off = 8 - (CONV_WIDTH - 1) + j
            acc = acc + cw_ref[j:j + 1, idx * width:(idx + 1) * width] * ext_ref[idx, off:off + ts, :]
        out_ref[0] = (_silu(acc) * sc).astype(BF16)
        ext_ref[idx, 0:8, :] = ext_ref[idx, ts:ts + 8, :]

    vml_ref[0] = proj(6).astype(BF16)
    gml_ref[0] = (_sigmoid(proj(7)) * _silu(proj(8))).astype(BF16)

    gt = lax.dot_general(wg_ref[...], hn, NT_DIMS, preferred_element_type=F32) + bg_ref[...]
    is_input_gate = lax.broadcasted_iota(jnp.int32, gt.shape, 0) < ML_HEADS
    gates_ref[0] = jnp.where(is_input_gate, gt, _log_sigmoid(gt))


def _inproj(x, mod, norm_gain, w_main, wg_t, b_gates, pmat, qg, kg, conv_w, conv_b):
    bsz, seq, d = x.shape
    width = d // 2
    ts = min(IN_TILE, seq)
    ngroups = w_main.shape[1] // width
    const2 = lambda b, s: (0, 0)
    tile_spec = pl.BlockSpec((1, ts, width), lambda b, s: (b, s, 0))
    act = jax.ShapeDtypeStruct((bsz, seq, width), BF16)
    kern = functools.partial(_inproj_kernel, width=width,
                             ml_scale=1.0 / math.sqrt(width // ML_HEADS))
    return pl.pallas_call(
        kern,
        grid=(bsz, seq // ts),
        in_specs=[pl.BlockSpec((1, ts, d), lambda b, s: (b, s, 0)),
                  pl.BlockSpec((3, 1, 1, d), lambda b, s: (0, b, 0, 0)),
                  pl.BlockSpec((1, d), const2),
                  pl.BlockSpec((d, ngroups * width), const2),
                  pl.BlockSpec((2 * ML_HEADS, d), const2),
                  pl.BlockSpec((2 * ML_HEADS, 1), const2),
                  pl.BlockSpec((width, width), const2),
                  pl.BlockSpec((1, width), const2),
                  pl.BlockSpec((1, width), const2),
                  pl.BlockSpec((CONV_WIDTH, 2 * width), const2),
                  pl.BlockSpec((1, 2 * width), const2)],
        out_specs=[tile_spec] * 8 + [pl.BlockSpec((1, 2 * ML_HEADS, ts), lambda b, s: (b, 0, s))],
        out_shape=[act] * 8 + [jax.ShapeDtypeStruct((bsz, 2 * ML_HEADS, seq), F32)],
        scratch_shapes=[pltpu.VMEM((2, ts + 8, width), F32)],
        compiler_params=pltpu.CompilerParams(dimension_semantics=("parallel", "arbitrary"),
                                             vmem_limit_bytes=VMEM_LIMIT),
        name="inproj",
    )(x, mod.reshape(3, bsz, 1, d), norm_gain.reshape(1, d), w_main, wg_t,
      b_gates.reshape(2 * ML_HEADS, 1), pmat, qg, kg, conv_w, conv_b.reshape(1, 2 * width))


def _sb_kernel(q_ref, k_ref, v_ref, g_ref, tri_ref, o_ref):
    seq = q_ref.shape[1]
    tq, tk = SB_TQ, SB_TK
    lane_a = lax.broadcasted_iota(jnp.int32, (tq, LANES), 1) < SB_HEAD_DIM
    rows = lax.broadcasted_iota(jnp.int32, (2 * tq, tk), 0)
    cols = lax.broadcasted_iota(jnp.int32, (2 * tq, tk), 1)
    strict = cols < jnp.where(rows < tq, rows, rows - tq)
    tri = tri_ref[...]

    def tile(qab, j, carry, masked):
        s0 = pl.multiple_of(j * tk, tk)
        k2 = k_ref[0, pl.ds(s0, tk), :]
        v2 = v_ref[0, pl.ds(s0, tk), :]
        z = lax.dot_general(qab, k2, NT_DIMS, preferred_element_type=F32)
        lp = jnp.log1p(jnp.exp(-jnp.abs(z)))
        log_1mb = -(jnp.maximum(z, 0.0) + lp)
        if masked:
            log_1mb = jnp.where(strict, log_1mb, 0.0)
        r = jnp.dot(log_1mb.astype(BF16), tri, preferred_element_type=F32)
        a = jnp.exp(jnp.minimum(z, 0.0) - lp + carry + r[:, :tk])
        if masked:
            a = jnp.where(strict, a, 0.0)
        pv = jnp.dot(a.astype(BF16), v2, preferred_element_type=F32)
        return carry + r[:, tk:], jnp.where(lane_a, pv[:tq], pv[tq:])

    def q_block(i, _):
        t0 = pl.multiple_of(i * tq, tq)
        q2 = q_ref[0, pl.ds(t0, tq), :]
        zero = jnp.zeros_like(q2)
        qab = jnp.concatenate([jnp.where(lane_a, q2, zero), jnp.where(lane_a, zero, q2)], axis=0)
        carry, acc = tile(qab, i, jnp.zeros((2 * tq, tk), F32), True)

        def cond(st):
            j, _, _, top = st
            return jnp.logical_and(j >= 0, top > SB_LOG_STICK_FLOOR)

        def body(st):
            j, carry, acc, _ = st
            carry, o = tile(qab, j, carry, False)
            return j - 1, carry, acc + o, jnp.max(carry)

        _, _, acc, _ = lax.while_loop(cond, body, (i - 1, carry, acc, jnp.max(carry)))
        gate = g_ref[0, pl.ds(t0, tq), :].astype(F32)
        o_ref[0, pl.ds(t0, tq), :] = (acc * gate).astype(BF16)
        return 0

    lax.fori_loop(0, seq // tq, q_block, 0)


def _sb_attention(q, k, v, g, tri):
    bsz, seq, width = q.shape
    spec = pl.BlockSpec((1, seq, LANES), lambda b, h: (b, 0, h))
    return pl.pallas_call(
        _sb_kernel,
        grid=(bsz, width // LANES),
        in_specs=[spec, spec, spec, spec, pl.BlockSpec(tri.shape, lambda b, h: (0, 0))],
        out_specs=spec,
        out_shape=jax.ShapeDtypeStruct((bsz, seq, width), BF16),
        compiler_params=pltpu.CompilerParams(dimension_semantics=("parallel", "parallel"),
                                             vmem_limit_bytes=VMEM_LIMIT),
        name="sb_attention",
    )(q, k, v, g, tri)


def _ml_kernel(q_ref, k_ref, v_ref, g_ref, ig_ref, fg_ref, gain_ref, o_ref):
    seq = q_ref.shape[1]
    dh = q_ref.shape[2]
    clen = ML_CHUNK
    row = lax.broadcasted_iota(jnp.int32, (clen, clen), 0)
    col = lax.broadcasted_iota(jnp.int32, (clen, clen), 1)
    causal = col <= row
    eye = col == row
    gain = gain_ref[...]

    def to_col(r):
        return jnp.sum(jnp.where(eye, r, 0.0), axis=1, keepdims=True)

    def chunk(c, state):
        c_mat, n_vec, m_prev = state
        t0 = pl.multiple_of(c * clen, clen)
        q = q_ref[0, pl.ds(t0, clen), :]
        k = k_ref[0, pl.ds(t0, clen), :]
        v = v_ref[0, pl.ds(t0, clen), :]
        i_row = ig_ref[0, 0, :, pl.ds(t0, clen)]
        f_row = fg_ref[0, 0, :, pl.ds(t0, clen)]
        bcum_col = jnp.sum(jnp.where(causal, f_row, 0.0), axis=1, keepdims=True)
        bcum_row = jnp.sum(jnp.where(row <= col, to_col(f_row), 0.0), axis=0, keepdims=True)
        b_last = jnp.sum(f_row, axis=1, keepdims=True)

        log_d = jnp.where(causal, bcum_col - bcum_row + i_row, -jnp.inf)
        inter = bcum_col + m_prev
        m_t = jnp.maximum(inter, jnp.max(log_d, axis=1, keepdims=True))
        d_mat = jnp.exp(log_d - m_t)
        g_inter = jnp.exp(inter - m_t)
        sc = lax.dot_general(q, k, NT_DIMS, preferred_element_type=F32) * d_mat
        num = (jnp.dot(sc.astype(BF16), v, preferred_element_type=F32)
               + g_inter * jnp.dot(q, c_mat.astype(BF16), preferred_element_type=F32))
        den = (jnp.sum(sc, axis=1, keepdims=True)
               + g_inter * jnp.sum(q.astype(F32) * n_vec, axis=1, keepdims=True))
        h_out = num / jnp.maximum(jnp.abs(den), jnp.exp(-m_t))

        hn = h_out * lax.rsqrt(jnp.mean(h_out * h_out, axis=1, keepdims=True) + EPS) * gain
        gate = g_ref[0, pl.ds(t0, clen), :].astype(F32)
        o_ref[0, pl.ds(t0, clen), :] = (hn * gate).astype(BF16)

        w_log = b_last - bcum_row + i_row
        m_new = jnp.maximum(b_last + m_prev, jnp.max(w_log, axis=1, keepdims=True))
        w_col = to_col(jnp.exp(w_log - m_new))
        g_c = jnp.exp(b_last + m_prev - m_new)
        kw = k.astype(F32) * w_col
        c_new = g_c * c_mat + lax.dot_general(kw.astype(BF16), v, TN_DIMS,
                                              preferred_element_type=F32)
        n_new = g_c * n_vec + jnp.sum(kw, axis=0, keepdims=True)
        return c_new, n_new, m_new

    init = (jnp.zeros((dh, dh), F32), jnp.zeros((1, dh), F32), jnp.zeros((1, 1), F32))
    lax.fori_loop(0, seq // clen, chunk, init)


def _mlstm(q, k, v, g, gates, gain):
    bsz, seq, width = q.shape
    dh = width // ML_HEADS
    gates4 = gates.reshape(bsz, 2 * ML_HEADS, 1, seq)
    spec = pl.BlockSpec((1, seq, dh), lambda b, h: (b, 0, h))
    return pl.pallas_call(
        _ml_kernel,
        grid=(bsz, ML_HEADS),
        in_specs=[spec, spec, spec, spec,
                  pl.BlockSpec((1, 1, 1, seq), lambda b, h: (b, h, 0, 0)),
                  pl.BlockSpec((1, 1, 1, seq), lambda b, h: (b, ML_HEADS + h, 0, 0)),
                  pl.BlockSpec((1, dh), lambda b, h: (0, h))],
        out_specs=spec,
        out_shape=jax.ShapeDtypeStruct((bsz, seq, width), BF16),
        compiler_params=pltpu.CompilerParams(dimension_semantics=("parallel", "parallel"),
                                             vmem_limit_bytes=VMEM_LIMIT),
        name="mlstm",
    )(q, k, v, g, gates4, gates4, gain.reshape(1, width))


def _outproj_kernel(ysb_ref, yml_ref, x_ref, mod_ref, w_ref, o_ref, *, width):
    y = (jnp.dot(ysb_ref[0], w_ref[:width, :], preferred_element_type=F32)
         + jnp.dot(yml_ref[0], w_ref[width:, :], preferred_element_type=F32))
    o_ref[0] = x_ref[0] + mod_ref[2, 0] * y


def _outproj(ysb, yml, x, mod, w_out):
    bsz, seq, d = x.shape
    width = ysb.shape[2]
    ts = min(OUT_TILE, seq)
    ytile = pl.BlockSpec((1, ts, width), lambda b, s: (b, s, 0))
    xtile = pl.BlockSpec((1, ts, d), lambda b, s: (b, s, 0))
    return pl.pallas_call(
        functools.partial(_outproj_kernel, width=width),
        grid=(bsz, seq // ts),
        in_specs=[ytile, ytile, xtile,
                  pl.BlockSpec((3, 1, 1, d), lambda b, s: (0, b, 0, 0)),
                  pl.BlockSpec(w_out.shape, lambda b, s: (0, 0))],
        out_specs=xtile,
        out_shape=jax.ShapeDtypeStruct(x.shape, x.dtype),
        compiler_params=pltpu.CompilerParams(dimension_semantics=("parallel", "parallel"),
                                             vmem_limit_bytes=VMEM_LIMIT),
        name="outproj",
    )(ysb, yml, x, mod.reshape(3, bsz, 1, d), w_out)


def _layer(x, c, w_ada, b_ada, norm_gain, w_in, b_gates, q_norm_gain, k_norm_gain,
           conv_w, conv_b, ml_norm_gain, w_out):
    bsz, seq, d = x.shape
    width = d // 2
    assert d % (2 * LANES) == 0 and width == ML_HEADS * LANES
    assert seq % IN_TILE == 0 and seq % SB_TQ == 0 and seq % ML_CHUNK == 0
    nmain = 9 * width
    w_main = w_in[:, :nmain].astype(BF16)
    wg_t = w_in[:, nmain:].T.astype(BF16)
    heads = width // SB_HEAD_DIM
    head_id = jnp.arange(width) // SB_HEAD_DIM
    pmat = jnp.where(head_id[:, None] == head_id[None, :], 1.0 / SB_HEAD_DIM, 0.0).astype(BF16)
    qg = jnp.tile(q_norm_gain, heads).reshape(1, width)
    kg = jnp.tile(k_norm_gain, heads).reshape(1, width)
    kk = jnp.arange(SB_TK)
    tri = jnp.concatenate([(kk[:, None] > kk[None, :]).astype(BF16),
                           jnp.ones((SB_TK, SB_TK), BF16)], axis=1)

    mod = _mod(c, w_ada, b_ada)
    (qsb, ksb, vsb, gsb, qml, kml, vml, gml, gates) = _inproj(
        x, mod, norm_gain, w_main, wg_t, b_gates, pmat, qg, kg, conv_w, conv_b)
    ysb = _sb_attention(qsb, ksb, vsb, gsb, tri)
    yml = _mlstm(qml, kml, vml, gml, gates, ml_norm_gain)
    return _outproj(ysb, yml, x, mod, w_out.astype(BF16))


def kernel(x, c, w_ada, b_ada, norm_gain, w_in, b_gates, q_norm_gain, k_norm_gain, conv_w,
           conv_b, ml_norm_gain, w_out):
    h = x
    for layer in range(w_in.shape[0]):
        h = _layer(h, c, w_ada[layer], b_ada[layer], norm_gain[layer], w_in[layer],
                   b_gates[layer], q_norm_gain[layer], k_norm_gain[layer], conv_w[layer],
                   conv_b[layer], ml_norm_gain[layer], w_out[layer])
    return h
```

```python
import functools
import math

import jax
import jax.numpy as jnp
from jax import lax
from jax.experimental import pallas as pl
from jax.experimental.pallas import tpu as pltpu

F32 = jnp.float32
BF16 = jnp.bfloat16

SB_HEAD_DIM = 64
ML_HEADS = 4
CONV_WIDTH = 4
EPS = 1e-6

LANES = 128
VMEM_LIMIT = 56 * 1024 * 1024

IN_TILE = 512
OUT_TILE = 512
SB_TILE = 128
SB_WIDE = 3
SB_UNROLL = 2
ML_CHUNK = 128
ML_TILE = 1024
SB_LOG_STICK_FLOOR = -104.0

NT_DIMS = (((1,), (1,)), ((), ()))
TN_DIMS = (((0,), (0,)), ((), ()))


def _silu(u):
    return u / (1.0 + jnp.exp(-u))


def _sigmoid(u):
    return 1.0 / (1.0 + jnp.exp(-u))


def _log_sigmoid(u):
    return jnp.minimum(u, 0.0) - jnp.log1p(jnp.exp(-jnp.abs(u)))


def _mod_kernel(c_ref, w_ref, b_ref, o_ref):
    c = c_ref[...]
    o_ref[0] = jnp.dot(_silu(c), w_ref[...], preferred_element_type=F32) + b_ref[0]


def _mod(c, w_ada, b_ada):
    bsz, d = c.shape
    return pl.pallas_call(
        _mod_kernel,
        grid=(3,),
        in_specs=[pl.BlockSpec((bsz, d), lambda j: (0, 0)),
                  pl.BlockSpec((d, d), lambda j: (0, j)),
                  pl.BlockSpec((1, 1, d), lambda j: (j, 0, 0))],
        out_specs=pl.BlockSpec((1, bsz, d), lambda j: (j, 0, 0)),
        out_shape=jax.ShapeDtypeStruct((3, bsz, d), F32),
        compiler_params=pltpu.CompilerParams(dimension_semantics=("arbitrary",),
                                             vmem_limit_bytes=VMEM_LIMIT),
        name="adaln_mod",
    )(c, w_ada, b_ada.reshape(3, 1, d))


def _inproj_kernel(x_ref, mod_ref, gain_ref, w_ref, wg_ref, bg_ref, p_ref, qg_ref, kg_ref,
                   cw_ref, cb_ref,
                   qsb_ref, ksb_ref, vsb_ref, gsb_ref, qml_ref, kml_ref, vml_ref, gml_ref,
                   gates_ref, ext_ref, *, width, ml_scale):
    s = pl.program_id(1)
    ts = x_ref.shape[1]
    x = x_ref[0]
    ms = jnp.mean(x * x, axis=-1, keepdims=True)
    shift = mod_ref[0, 0]
    scale = mod_ref[1, 0]
    hn = (x * lax.rsqrt(ms + EPS)) * (gain_ref[...] * (1.0 + scale)) + shift
    hn = hn.astype(BF16)

    def proj(g):
        return jnp.dot(hn, w_ref[:, g * width:(g + 1) * width], preferred_element_type=F32)

    for g, gn_ref, out_ref, sc in ((0, qg_ref, qsb_ref, 1.0 / math.sqrt(SB_HEAD_DIM)),
                                   (1, kg_ref, ksb_ref, 1.0)):
        u = proj(g)
        msh = jnp.dot((u * u).astype(BF16), p_ref[...], preferred_element_type=F32)
        out_ref[0] = (u * lax.rsqrt(msh + EPS) * (gn_ref[...] * sc)).astype(BF16)
    vsb_ref[0] = proj(2).astype(BF16)
    gsb_ref[0] = _silu(proj(3)).astype(BF16)

    @pl.when(s == 0)
    def _():
        ext_ref[:, 0:8, :] = jnp.zeros((2, 8, width), F32)

    for idx, (g, out_ref, sc) in enumerate(((4, qml_ref, ml_scale), (5, kml_ref, 1.0))):
        ext_ref[idx, 8:, :] = proj(g)
        acc = cb_ref[:, idx * width:(idx + 1) * width]
        for j in range(CONV_WIDTH):
            off = 8 - (CONV_WIDTH - 1) + j
            acc = acc + cw_ref[j:j + 1, idx * width:(idx + 1) * width] * ext_ref[idx, off:off + ts, :]
        out_ref[0] = (_silu(acc) * sc).astype(BF16)
        ext_ref[idx, 0:8, :] = ext_ref[idx, ts:ts + 8, :]

    vml_ref[0] = proj(6).astype(BF16)
    gml_ref[0] = (_sigmoid(proj(7)) * _silu(proj(8))).astype(BF16)

    gt = lax.dot_general(wg_ref[...], hn, NT_DIMS, preferred_element_type=F32) + bg_ref[...]
    is_input_gate = lax.broadcasted_iota(jnp.int32, gt.shape, 0) < ML_HEADS
    gates_ref[0] = jnp.where(is_input_gate, gt, _log_sigmoid(gt))


def _inproj(x, mod, norm_gain, w_main, wg_t, b_gates, pmat, qg, kg, conv_w, conv_b):
    bsz, seq, d = x.shape
    width = d // 2
    ts = min(IN_TILE, seq)
    ngroups = w_main.shape[1] // width
    const2 = lambda b, s: (0, 0)
    tile_spec = pl.BlockSpec((1, ts, width), lambda b, s: (b, s, 0))
    act = jax.ShapeDtypeStruct((bsz, seq, width), BF16)
    kern = functools.partial(_inproj_kernel, width=width,
                             ml_scale=1.0 / math.sqrt(width // ML_HEADS))
    return pl.pallas_call(
        kern,
        grid=(bsz, seq // ts),
        in_specs=[pl.BlockSpec((1, ts, d), lambda b, s: (b, s, 0)),
                  pl.BlockSpec((3, 1, 1, d), lambda b, s: (0, b, 0, 0)),
                  pl.BlockSpec((1, d), const2),
                  pl.BlockSpec((d, ngroups * width), const2),
                  pl.BlockSpec((2 * ML_HEADS, d), const2),
                  pl.BlockSpec((2 * ML_HEADS, 1), const2),
                  pl.BlockSpec((width, width), const2),
                  pl.BlockSpec((1, width), const2),
                  pl.BlockSpec((1, width), const2),
                  pl.BlockSpec((CONV_WIDTH, 2 * width), const2),
                  pl.BlockSpec((1, 2 * width), const2)],
        out_specs=[tile_spec] * 8 + [pl.BlockSpec((1, 2 * ML_HEADS, ts), lambda b, s: (b, 0, s))],
        out_shape=[act] * 8 + [jax.ShapeDtypeStruct((bsz, 2 * ML_HEADS, seq), F32)],
        scratch_shapes=[pltpu.VMEM((2, ts + 8, width), F32)],
        compiler_params=pltpu.CompilerParams(dimension_semantics=("parallel", "arbitrary"),
                                             vmem_limit_bytes=VMEM_LIMIT),
        name="inproj",
    )(x, mod.reshape(3, bsz, 1, d), norm_gain.reshape(1, d), w_main, wg_t,
      b_gates.reshape(2 * ML_HEADS, 1), pmat, qg, kg, conv_w, conv_b.reshape(1, 2 * width))


def _sb_kernel(q_ref, k_ref, v_ref, g_ref, tri_ref, o_ref):
    seq = q_ref.shape[1]
    tq = tk = SB_TILE
    wk = SB_WIDE * tk
    lane_a = lax.broadcasted_iota(jnp.int32, (tq, LANES), 1) < SB_HEAD_DIM
    rows = lax.broadcasted_iota(jnp.int32, (2 * tq, tk), 0)
    cols = lax.broadcasted_iota(jnp.int32, (2 * tq, tk), 1)
    rel = cols - jnp.where(rows < tq, rows, rows - tq)
    diag_strict = rel < 0
    ntri = tri_ref[...]

    def softplus(z):
        return jnp.maximum(z, 0.0) + jnp.log(1.0 + jnp.exp(-jnp.abs(z)))

    def key_block(z, sp, carry, mask):
        sp_in = sp if mask is None else jnp.where(mask, sp, 0.0)
        r = jnp.dot(sp_in.astype(BF16), ntri, preferred_element_type=F32)
        a = jnp.exp(z - sp + carry + r[:, :tk])
        if mask is not None:
            a = jnp.where(mask, a, 0.0)
        return a.astype(BF16), carry + r[:, tk:]

    def split_heads(pv):
        return jnp.where(lane_a, pv[:tq], pv[tq:])

    def wide_step(i, diag_only):
        t0 = pl.multiple_of(i * tq, tq)
        q2 = q_ref[0, pl.ds(t0, tq), :]
        zero = jnp.zeros_like(q2)
        qab = jnp.concatenate([jnp.where(lane_a, q2, zero), jnp.where(lane_a, zero, q2)], axis=0)

        first = i - (SB_WIDE - 1) if diag_only else jnp.maximum(i - (SB_WIDE - 1), 0)
        start = pl.multiple_of(first * tk, tk)
        z = lax.dot_general(qab, k_ref[0, pl.ds(start, wk), :], NT_DIMS,
                            preferred_element_type=F32)
        sp = softplus(z)
        carry = jnp.zeros((2 * tq, tk), F32)
        a_parts = [None] * SB_WIDE
        for m in reversed(range(SB_WIDE)):
            if diag_only:
                mask = diag_strict if m == SB_WIDE - 1 else None
            else:
                mask = rel < (t0 - start - m * tk)
            sl = slice(m * tk, (m + 1) * tk)
            a_parts[m], carry = key_block(z[:, sl], sp[:, sl], carry, mask)
        acc = split_heads(jnp.dot(jnp.concatenate(a_parts, axis=1), v_ref[0, pl.ds(start, wk), :],
                                  preferred_element_type=F32))
        return qab, first, carry, acc, jnp.max(carry)

    def finish(i, qab, first, carry, acc, top):
        t0 = pl.multiple_of(i * tq, tq)

        def cond(st):
            j, _, _, top = st
            return jnp.logical_and(j >= 0, top > SB_LOG_STICK_FLOOR)

        def body(st):
            j, carry, acc, _ = st
            s0 = pl.multiple_of(j * tk, tk)
            z = lax.dot_general(qab, k_ref[0, pl.ds(s0, tk), :], NT_DIMS,
                                preferred_element_type=F32)
            a, carry = key_block(z, softplus(z), carry, None)
            pv = jnp.dot(a, v_ref[0, pl.ds(s0, tk), :], preferred_element_type=F32)
            return j - 1, carry, acc + split_heads(pv), jnp.max(carry)

        _, _, acc, _ = lax.while_loop(cond, body, (first - 1, carry, acc, top))
        gate = g_ref[0, pl.ds(t0, tq), :].astype(F32)
        o_ref[0, pl.ds(t0, tq), :] = (acc * gate).astype(BF16)

    def run(lo, hi, count, diag_only):
        def step(p, _):
            i0 = lo + p * count
            states = [wide_step(i0 + u, diag_only) for u in range(count)]
            for u, st in enumerate(states):
                finish(i0 + u, *st)
            return 0
        if hi > lo:
            lax.fori_loop(0, (hi - lo) // count, step, 0)

    nq = seq // tq
    n_lead = SB_WIDE - 1 + (nq - (SB_WIDE - 1)) % SB_UNROLL
    run(0, SB_WIDE - 1, 1, False)
    run(SB_WIDE - 1, n_lead, 1, True)
    run(n_lead, nq, SB_UNROLL, True)


def _sb_attention(q, k, v, g, tri):
    bsz, seq, width = q.shape
    spec = pl.BlockSpec((1, seq, LANES), lambda b, h: (b, 0, h))
    return pl.pallas_call(
        _sb_kernel,
        grid=(bsz, width // LANES),
        in_specs=[spec, spec, spec, spec, pl.BlockSpec(tri.shape, lambda b, h: (0, 0))],
        out_specs=spec,
        out_shape=jax.ShapeDtypeStruct((bsz, seq, width), BF16),
        compiler_params=pltpu.CompilerParams(dimension_semantics=("parallel", "parallel"),
                                             vmem_limit_bytes=VMEM_LIMIT),
        name="sb_attention",
    )(q, k, v, g, tri)


def _ml_kernel(q_ref, k_ref, v_ref, g_ref, gates_ref, gain_ref, o_ref, cn_ref, m_ref):
    ts = q_ref.shape[1]
    dh = q_ref.shape[2] // ML_HEADS
    clen = ML_CHUNK
    row = lax.broadcasted_iota(jnp.int32, (clen, clen), 0)
    col = lax.broadcasted_iota(jnp.int32, (clen, clen), 1)
    causal = col <= row
    eye = col == row
    ones_blk = jnp.ones((clen, dh), BF16)
    mean_mat = jnp.full((dh, dh), 1.0 / dh, BF16)

    @pl.when(pl.program_id(1) == 0)
    def _():
        cn_ref[...] = jnp.zeros(cn_ref.shape, F32)
        m_ref[...] = jnp.zeros(m_ref.shape, F32)

    def head_chunk(h, t0):
        lanes = slice(h * dh, (h + 1) * dh)
        q = q_ref[0, pl.ds(t0, clen), lanes]
        k = k_ref[0, pl.ds(t0, clen), lanes]
        v_aug = jnp.concatenate([v_ref[0, pl.ds(t0, clen), lanes], ones_blk], axis=1)
        i_row = gates_ref[0, h:h + 1, pl.ds(t0, clen)]
        f_row = gates_ref[0, ML_HEADS + h:ML_HEADS + h + 1, pl.ds(t0, clen)]
        cn = cn_ref[h]
        m_prev = m_ref[h]

        bcum_col = jnp.sum(jnp.where(causal, f_row, 0.0), axis=1, keepdims=True)
        bcum_row = jnp.sum(jnp.where(eye, bcum_col, 0.0), axis=0, keepdims=True)
        i_col = jnp.sum(jnp.where(eye, i_row, 0.0), axis=1, keepdims=True)
        b_last = jnp.sum(f_row, axis=1, keepdims=True)

        log_d = jnp.where(causal, bcum_col - bcum_row + i_row, -jnp.inf)
        inter = bcum_col + m_prev
        m_t = jnp.maximum(inter, jnp.max(log_d, axis=1, keepdims=True))
        d_mat = jnp.exp(log_d - m_t)
        g_inter = jnp.exp(inter - m_t)
        sc = lax.dot_general(q, k, NT_DIMS, preferred_element_type=F32) * d_mat
        tot = (jnp.dot(sc.astype(BF16), v_aug, preferred_element_type=F32)
               + g_inter * jnp.dot(q, cn.astype(BF16), preferred_element_type=F32))
        h_out = tot[:, :dh] / jnp.maximum(jnp.abs(tot[:, dh:]), jnp.exp(-m_t))

        ms = jnp.dot((h_out * h_out).astype(BF16), mean_mat, preferred_element_type=F32)
        hn = h_out * lax.rsqrt(ms + EPS) * gain_ref[:, lanes]
        gate = g_ref[0, pl.ds(t0, clen), lanes].astype(F32)
        o_ref[0, pl.ds(t0, clen), lanes] = (hn * gate).astype(BF16)

        w_log = b_last - bcum_row + i_row
        m_new = jnp.maximum(b_last + m_prev, jnp.max(w_log, axis=1, keepdims=True))
        w_col = jnp.exp(b_last - bcum_col + i_col - m_new)
        g_c = jnp.exp(b_last + m_prev - m_new)
        kw = (k.astype(F32) * w_col).astype(BF16)
        cn_ref[h] = g_c * cn + lax.dot_general(kw, v_aug, TN_DIMS, preferred_element_type=F32)
        m_ref[h] = m_new

    def chunk(c, _):
        t0 = pl.multiple_of(c * clen, clen)
        for h in range(ML_HEADS):
            head_chunk(h, t0)
        return 0

    lax.fori_loop(0, ts // clen, chunk, 0)


def _mlstm(q, k, v, g, gates, gain):
    bsz, seq, width = q.shape
    dh = width // ML_HEADS
    ts = min(ML_TILE, seq)
    spec = pl.BlockSpec((1, ts, width), lambda b, s: (b, s, 0))
    return pl.pallas_call(
        _ml_kernel,
        grid=(bsz, seq // ts),
        in_specs=[spec, spec, spec, spec,
                  pl.BlockSpec((1, 2 * ML_HEADS, ts), lambda b, s: (b, 0, s)),
                  pl.BlockSpec((1, width), lambda b, s: (0, 0))],
        out_specs=spec,
        out_shape=jax.ShapeDtypeStruct((bsz, seq, width), BF16),
        scratch_shapes=[pltpu.VMEM((ML_HEADS, dh, 2 * dh), F32),
                        pltpu.VMEM((ML_HEADS, 1, 1), F32)],
        compiler_params=pltpu.CompilerParams(dimension_semantics=("parallel", "arbitrary"),
                                             vmem_limit_bytes=VMEM_LIMIT),
        name="mlstm",
    )(q, k, v, g, gates, gain.reshape(1, width))


def _outproj_kernel(ysb_ref, yml_ref, x_ref, mod_ref, w_ref, o_ref, *, width):
    y = (jnp.dot(ysb_ref[0], w_ref[:width, :], preferred_element_type=F32)
         + jnp.dot(yml_ref[0], w_ref[width:, :], preferred_element_type=F32))
    o_ref[0] = x_ref[0] + mod_ref[2, 0] * y


def _outproj(ysb, yml, x, mod, w_out):
    bsz, seq, d = x.shape
    width = ysb.shape[2]
    ts = min(OUT_TILE, seq)
    ytile = pl.BlockSpec((1, ts, width), lambda b, s: (b, s, 0))
    xtile = pl.BlockSpec((1, ts, d), lambda b, s: (b, s, 0))
    return pl.pallas_call(
        functools.partial(_outproj_kernel, width=width),
        grid=(bsz, seq // ts),
        in_specs=[ytile, ytile, xtile,
                  pl.BlockSpec((3, 1, 1, d), lambda b, s: (0, b, 0, 0)),
                  pl.BlockSpec(w_out.shape, lambda b, s: (0, 0))],
        out_specs=xtile,
        out_shape=jax.ShapeDtypeStruct(x.shape, x.dtype),
        compiler_params=pltpu.CompilerParams(dimension_semantics=("parallel", "parallel"),
                                             vmem_limit_bytes=VMEM_LIMIT),
        name="outproj",
    )(ysb, yml, x, mod.reshape(3, bsz, 1, d), w_out)


def _layer(x, c, w_ada, b_ada, norm_gain, w_in, b_gates, q_norm_gain, k_norm_gain,
           conv_w, conv_b, ml_norm_gain, w_out):
    bsz, seq, d = x.shape
    width = d // 2
    assert d % (2 * LANES) == 0 and width == ML_HEADS * LANES
    assert seq % IN_TILE == 0 and seq % SB_TILE == 0 and seq % ML_TILE == 0
    assert ML_TILE % ML_CHUNK == 0
    assert seq >= SB_WIDE * SB_TILE
    nmain = 9 * width
    w_main = w_in[:, :nmain].astype(BF16)
    wg_t = w_in[:, nmain:].T.astype(BF16)
    heads = width // SB_HEAD_DIM
    head_id = jnp.arange(width) // SB_HEAD_DIM
    pmat = jnp.where(head_id[:, None] == head_id[None, :], 1.0 / SB_HEAD_DIM, 0.0).astype(BF16)
    qg = jnp.tile(q_norm_gain, heads).reshape(1, width)
    kg = jnp.tile(k_norm_gain, heads).reshape(1, width)
    kk = jnp.arange(SB_TILE)
    tri = -jnp.concatenate([(kk[:, None] > kk[None, :]).astype(BF16),
                            jnp.ones((SB_TILE, SB_TILE), BF16)], axis=1)

    mod = _mod(c, w_ada, b_ada)
    (qsb, ksb, vsb, gsb, qml, kml, vml, gml, gates) = _inproj(
        x, mod, norm_gain, w_main, wg_t, b_gates, pmat, qg, kg, conv_w, conv_b)
    ysb = _sb_attention(qsb, ksb, vsb, gsb, tri)
    yml = _mlstm(qml, kml, vml, gml, gates, ml_norm_gain)
    return _outproj(ysb, yml, x, mod, w_out.astype(BF16))


def kernel(x, c, w_ada, b_ada, norm_gain, w_in, b_gates, q_norm_gain, k_norm_gain, conv_w,
           conv_b, ml_norm_gain, w_out):
    h = x
    for layer in range(w_in.shape[0]):
        h = _layer(h, c, w_ada[layer], b_ada[layer], norm_gain[layer], w_in[layer],
                   b_gates[layer], q_norm_gain[layer], k_norm_gain[layer], conv_w[layer],
                   conv_b[layer], ml_norm_gain[layer], w_out[layer])
    return h
```

```python
import functools
import math

import jax
import jax.numpy as jnp
from jax import lax
from jax.experimental import pallas as pl
from jax.experimental.pallas import tpu as pltpu

F32 = jnp.float32
BF16 = jnp.bfloat16

SB_HEAD_DIM = 64
ML_HEADS = 4
CONV_WIDTH = 4
EPS = 1e-6

LANES = 128
SUBLANES = 8
GATE_ROWS = SUBLANES
GATE_GROUPS = 5
GATE_SCAN_SPAN = 256
VMEM_LIMIT = 56 * 1024 * 1024

IN_TILE = 512
OUT_TILE = 512
SB_TILE = 128
SB_WIDE = 3
SB_UNROLL = 3
ML_CHUNK = 128
ML_TILE = 1024
ML_MAX_EXP_ARG = 88.0
SB_LOG2_STICK_FLOOR = -150.0

NT_DIMS = (((1,), (1,)), ((), ()))
TN_DIMS = (((0,), (0,)), ((), ()))


def _silu(u):
    return u / (1.0 + jnp.exp(-u))


def _sigmoid(u):
    return 1.0 / (1.0 + jnp.exp(-u))


def _log_sigmoid(u):
    return jnp.minimum(u, 0.0) - jnp.log1p(jnp.exp(-jnp.abs(u)))


def _mod_kernel(c_ref, w_ref, b_ref, o_ref):
    c = c_ref[...]
    o_ref[0] = jnp.dot(_silu(c), w_ref[...], preferred_element_type=F32) + b_ref[0]


def _mod(c, w_ada, b_ada):
    bsz, d = c.shape
    return pl.pallas_call(
        _mod_kernel,
        grid=(3,),
        in_specs=[pl.BlockSpec((bsz, d), lambda j: (0, 0)),
                  pl.BlockSpec((d, d), lambda j: (0, j)),
                  pl.BlockSpec((1, 1, d), lambda j: (j, 0, 0))],
        out_specs=pl.BlockSpec((1, bsz, d), lambda j: (j, 0, 0)),
        out_shape=jax.ShapeDtypeStruct((3, bsz, d), F32),
        compiler_params=pltpu.CompilerParams(dimension_semantics=("arbitrary",),
                                             vmem_limit_bytes=VMEM_LIMIT),
        name="adaln_mod",
    )(c, w_ada, b_ada.reshape(3, 1, d))


def _split3(x):
    p0 = x.astype(BF16).astype(F32)
    r1 = x - p0
    p1 = r1.astype(BF16).astype(F32)
    return jnp.concatenate([p0, p1, r1 - p1, jnp.zeros_like(x)], axis=0).astype(BF16)


def _exact_rows_dot(x, mat):
    r = x.shape[0]
    y = jnp.dot(_split3(x), mat, preferred_element_type=F32)
    return y[0:r] + y[r:2 * r] + y[2 * r:3 * r]


def _inproj_kernel(x_ref, mod_ref, gain_ref, w_ref, wg_ref, bg_ref, p_ref, qg_ref, kg_ref,
                   cw_ref, cb_ref,
                   qsb_ref, ksb_ref, vsb_ref, gsb_ref, qml_ref, kml_ref, vml_ref, gml_ref,
                   gates_ref, ext_ref, *, width, ml_scale):
    ts = x_ref.shape[1]

    @pl.when(pl.program_id(1) == 0)
    def _():
        ext_ref[:, 0:8, :] = jnp.zeros((2, 8, width), F32)

    x = x_ref[0]
    ms = jnp.mean(x * x, axis=-1, keepdims=True)
    shift = mod_ref[0, 0]
    scale = mod_ref[1, 0]
    hn = (x * lax.rsqrt(ms + EPS)) * (gain_ref[...] * (1.0 + scale)) + shift
    hn = hn.astype(BF16)

    gates_ref[0] = (lax.dot_general(wg_ref[...], hn, NT_DIMS, preferred_element_type=F32)
                    + bg_ref[...])

    u = [jnp.dot(hn, w_ref[:, g * width:(g + 1) * width], preferred_element_type=F32)
         for g in range(9)]

    msh = [jnp.dot((u[g] * u[g]).astype(BF16), p_ref[...], preferred_element_type=F32)
           for g in (0, 1)]
    for g, gn_ref, out_ref, sc in ((0, qg_ref, qsb_ref, math.log2(math.e) / math.sqrt(SB_HEAD_DIM)),
                                   (1, kg_ref, ksb_ref, 1.0)):
        out_ref[0] = (u[g] * lax.rsqrt(msh[g] + EPS) * (gn_ref[...] * sc)).astype(BF16)
    vsb_ref[0] = u[2].astype(BF16)
    gsb_ref[0] = _silu(u[3]).astype(BF16)

    for idx, (g, out_ref, sc) in enumerate(((4, qml_ref, ml_scale), (5, kml_ref, 1.0))):
        ext_ref[idx, 8:, :] = u[g]
        acc = cb_ref[:, idx * width:(idx + 1) * width]
        for j in range(CONV_WIDTH):
            off = 8 - (CONV_WIDTH - 1) + j
            acc = acc + cw_ref[j:j + 1, idx * width:(idx + 1) * width] * ext_ref[idx, off:off + ts, :]
        out_ref[0] = (_silu(acc) * sc).astype(BF16)
        ext_ref[idx, 0:8, :] = ext_ref[idx, ts:ts + 8, :]

    vml_ref[0] = u[6].astype(BF16)
    gml_ref[0] = (_sigmoid(u[7]) * _silu(u[8])).astype(BF16)


def _inproj(x, mod, norm_gain, w_main, wg_t, b_gates, pmat, qg, kg, conv_w, conv_b):
    bsz, seq, d = x.shape
    width = d // 2
    ts = min(IN_TILE, seq)
    ngroups = w_main.shape[1] // width
    const2 = lambda b, s: (0, 0)
    tile_spec = pl.BlockSpec((1, ts, width), lambda b, s: (b, s, 0))
    act = jax.ShapeDtypeStruct((bsz, seq, width), BF16)
    kern = functools.partial(_inproj_kernel, width=width,
                             ml_scale=1.0 / math.sqrt(width // ML_HEADS))
    return pl.pallas_call(
        kern,
        grid=(bsz, seq // ts),
        in_specs=[pl.BlockSpec((1, ts, d), lambda b, s: (b, s, 0)),
                  pl.BlockSpec((3, 1, 1, d), lambda b, s: (0, b, 0, 0)),
                  pl.BlockSpec((1, d), const2),
                  pl.BlockSpec((d, ngroups * width), const2),
                  pl.BlockSpec((2 * GATE_ROWS, d), const2),
                  pl.BlockSpec((2 * GATE_ROWS, 1), const2),
                  pl.BlockSpec((width, width), const2),
                  pl.BlockSpec((1, width), const2),
                  pl.BlockSpec((1, width), const2),
                  pl.BlockSpec((CONV_WIDTH, 2 * width), const2),
                  pl.BlockSpec((1, 2 * width), const2)],
        out_specs=[tile_spec] * 8 + [pl.BlockSpec((1, 2 * GATE_ROWS, ts), lambda b, s: (b, 0, s))],
        out_shape=[act] * 8 + [jax.ShapeDtypeStruct((bsz, 2 * GATE_ROWS, seq), F32)],
        scratch_shapes=[pltpu.VMEM((2, ts + 8, width), F32)],
        compiler_params=pltpu.CompilerParams(dimension_semantics=("parallel", "arbitrary"),
                                             vmem_limit_bytes=VMEM_LIMIT),
        name="inproj",
    )(x, mod.reshape(3, bsz, 1, d), norm_gain.reshape(1, d), w_main, wg_t,
      b_gates, pmat, qg, kg, conv_w, conv_b.reshape(1, 2 * width))


def _gate_scan_kernel(gt_ref, scan_ref, o_ref):
    seq = gt_ref.shape[2]
    span = scan_ref.shape[1]

    def per_span(x, mat):
        return jnp.concatenate([_exact_rows_dot(x[:, j:j + span], mat)
                                for j in range(0, seq, span)], axis=1)

    gt = gt_ref[0]
    bcum = per_span(_log_sigmoid(gt[GATE_ROWS:]), scan_ref[0])
    u = gt[:GATE_ROWS] - bcum
    lane_in_chunk = lax.broadcasted_iota(jnp.int32, (GATE_ROWS, seq), 1) % ML_CHUNK
    u_max = u
    step = 1
    while step < ML_CHUNK:
        shifted = jnp.where(lane_in_chunk >= step, pltpu.roll(u_max, step, 1), -jnp.inf)
        u_max = jnp.maximum(u_max, shifted)
        step *= 2
    o_ref[0, 0:GATE_ROWS] = u
    o_ref[0, GATE_ROWS:2 * GATE_ROWS] = bcum
    o_ref[0, 2 * GATE_ROWS:3 * GATE_ROWS] = u_max
    o_ref[0, 3 * GATE_ROWS:5 * GATE_ROWS] = per_span(jnp.concatenate([bcum, u_max], axis=0),
                                                     scan_ref[1])


def _gate_scan(gt):
    bsz, rows, seq = gt.shape
    span = min(GATE_SCAN_SPAN, seq)
    t = jnp.arange(span)
    same_chunk = t[:, None] // ML_CHUNK == t[None, :] // ML_CHUNK
    scan_mats = jnp.stack([same_chunk & (t[:, None] <= t[None, :]),
                           same_chunk & (t[:, None] % ML_CHUNK == ML_CHUNK - 1)]).astype(BF16)
    return pl.pallas_call(
        _gate_scan_kernel,
        grid=(bsz,),
        in_specs=[pl.BlockSpec((1, rows, seq), lambda b: (b, 0, 0)),
                  pl.BlockSpec((2, span, span), lambda b: (0, 0, 0))],
        out_specs=pl.BlockSpec((1, GATE_GROUPS * GATE_ROWS, seq), lambda b: (b, 0, 0)),
        out_shape=jax.ShapeDtypeStruct((bsz, GATE_GROUPS * GATE_ROWS, seq), F32),
        compiler_params=pltpu.CompilerParams(dimension_semantics=("parallel",),
                                             vmem_limit_bytes=VMEM_LIMIT),
        name="gate_scan",
    )(gt, scan_mats)


def _sb_kernel(q_ref, k_ref, v_ref, g_ref, tri_ref, o_ref):
    seq = q_ref.shape[1]
    tq = tk = SB_TILE
    wk = SB_WIDE * tk
    lane_a = lax.broadcasted_iota(jnp.int32, (tq, LANES), 1) < SB_HEAD_DIM
    rows = lax.broadcasted_iota(jnp.int32, (2 * tq, tk), 0)
    cols = lax.broadcasted_iota(jnp.int32, (2 * tq, tk), 1)
    rel = cols - jnp.where(rows < tq, rows, rows - tq)
    diag_strict = rel < 0
    ntri = tri_ref[...]

    def softplus(z):
        return jnp.maximum(z, 0.0) + jnp.log2(1.0 + jnp.exp2(-jnp.abs(z)))

    def key_block(z, sp, carry, mask):
        sp_in = sp if mask is None else jnp.where(mask, sp, 0.0)
        r = jnp.dot(sp_in.astype(BF16), ntri, preferred_element_type=F32)
        a = jnp.exp2(z - sp + carry + r[:, :tk])
        if mask is not None:
            a = jnp.where(mask, a, 0.0)
        return a.astype(BF16), carry + r[:, tk:]

    def split_heads(pv):
        return jnp.where(lane_a, pv[:tq], pv[tq:])

    def wide_steps(i0, count, diag_only):
        blocks = range(count)
        qab, first, start, z = [], [], [], []
        for u in blocks:
            i = i0 + u
            q2 = q_ref[0, pl.ds(pl.multiple_of(i * tq, tq), tq), :]
            zero = jnp.zeros_like(q2)
            qab.append(jnp.concatenate([jnp.where(lane_a, q2, zero), jnp.where(lane_a, zero, q2)],
                                       axis=0))
            first.append(i - (SB_WIDE - 1) if diag_only else jnp.maximum(i - (SB_WIDE - 1), 0))
            start.append(pl.multiple_of(first[u] * tk, tk))
            z.append(lax.dot_general(qab[u], k_ref[0, pl.ds(start[u], wk), :], NT_DIMS,
                                     preferred_element_type=F32))
        sp = [softplus(z[u]) for u in blocks]
        carry = [jnp.zeros((2 * tq, tk), F32) for _ in blocks]
        a_parts = [[None] * SB_WIDE for _ in blocks]
        for m in reversed(range(SB_WIDE)):
            sl = slice(m * tk, (m + 1) * tk)
            for u in blocks:
                if diag_only:
                    mask = diag_strict if m == SB_WIDE - 1 else None
                else:
                    mask = rel < ((i0 + u) * tq - start[u] - m * tk)
                a_parts[u][m], carry[u] = key_block(z[u][:, sl], sp[u][:, sl], carry[u], mask)
        acc = [split_heads(jnp.dot(jnp.concatenate(a_parts[u], axis=1),
                                   v_ref[0, pl.ds(start[u], wk), :], preferred_element_type=F32))
               for u in blocks]
        return [(qab[u], first[u], carry[u], acc[u], jnp.max(carry[u])) for u in blocks]

    def finish(i, qab, first, carry, acc, top):
        t0 = pl.multiple_of(i * tq, tq)

        def cond(st):
            j, _, _, top = st
            return jnp.logical_and(j >= 0, top > SB_LOG2_STICK_FLOOR)

        def body(st):
            j, carry, acc, _ = st
            s0 = pl.multiple_of(j * tk, tk)
            z = lax.dot_general(qab, k_ref[0, pl.ds(s0, tk), :], NT_DIMS,
                                preferred_element_type=F32)
            a, carry = key_block(z, softplus(z), carry, None)
            pv = jnp.dot(a, v_ref[0, pl.ds(s0, tk), :], preferred_element_type=F32)
            return j - 1, carry, acc + split_heads(pv), jnp.max(carry)

        _, _, acc, _ = lax.while_loop(cond, body, (first - 1, carry, acc, top))
        gate = g_ref[0, pl.ds(t0, tq), :].astype(F32)
        o_ref[0, pl.ds(t0, tq), :] = (acc * gate).astype(BF16)

    def run(lo, hi, count, diag_only):
        def step(p, _):
            i0 = lo + p * count
            for u, st in enumerate(wide_steps(i0, count, diag_only)):
                finish(i0 + u, *st)
            return 0
        if hi > lo:
            lax.fori_loop(0, (hi - lo) // count, step, 0)

    nq = seq // tq
    n_lead = SB_WIDE - 1 + (nq - (SB_WIDE - 1)) % SB_UNROLL
    run(0, SB_WIDE - 1, 1, False)
    run(SB_WIDE - 1, n_lead, 1, True)
    run(n_lead, nq, SB_UNROLL, True)


def _sb_attention(q, k, v, g, tri):
    bsz, seq, width = q.shape
    spec = pl.BlockSpec((1, seq, LANES), lambda b, h: (b, 0, h))
    return pl.pallas_call(
        _sb_kernel,
        grid=(bsz, width // LANES),
        in_specs=[spec, spec, spec, spec, pl.BlockSpec(tri.shape, lambda b, h: (0, 0))],
        out_specs=spec,
        out_shape=jax.ShapeDtypeStruct((bsz, seq, width), BF16),
        compiler_params=pltpu.CompilerParams(dimension_semantics=("parallel", "parallel"),
                                             vmem_limit_bytes=VMEM_LIMIT),
        name="sb_attention",
    )(q, k, v, g, tri)


def _ml_kernel(q_ref, k_ref, v_ref, g_ref, gates_ref, gain_ref, sel_ref, o_ref,
               cn_ref, m_ref, x_ref):
    ts = q_ref.shape[1]
    nh = ML_HEADS
    dh = q_ref.shape[2] // nh
    clen = ML_CHUNK
    row = lax.broadcasted_iota(jnp.int32, (clen, clen), 0)
    col = lax.broadcasted_iota(jnp.int32, (clen, clen), 1)
    causal = col <= row
    ones_blk = jnp.ones((clen, dh), BF16)
    mean_mat = jnp.full((dh, dh), 1.0 / dh, BF16)

    @pl.when(pl.program_id(1) == 0)
    def _():
        cn_ref[...] = jnp.zeros(cn_ref.shape, F32)
        m_ref[...] = jnp.zeros(m_ref.shape, F32)

    def chunk(c, _):
        t0 = pl.multiple_of(c * clen, clen)
        heads = range(nh)
        lanes = [slice(h * dh, (h + 1) * dh) for h in heads]

        def gate_rows(group):
            return gates_ref[0, group * GATE_ROWS:group * GATE_ROWS + nh, pl.ds(t0, clen)]

        q = [q_ref[0, pl.ds(t0, clen), lanes[h]] for h in heads]
        k = [k_ref[0, pl.ds(t0, clen), lanes[h]] for h in heads]
        cn = [cn_ref[h] for h in heads]
        qk = [lax.dot_general(q[h], k[h], NT_DIMS, preferred_element_type=F32) for h in heads]
        inter = [jnp.dot(q[h], cn[h].astype(BF16), preferred_element_type=F32) for h in heads]

        u, bcum, u_max, b_last, u_top = (gate_rows(g) for g in range(GATE_GROUPS))
        m_prev = m_ref[...]
        m_rel = jnp.maximum(m_prev, u_max)
        m_new = b_last + jnp.maximum(m_prev, u_top)
        g_c = jnp.exp(b_last + m_prev - m_new)
        m_ref[...] = m_new
        x_ref[0 * nh:1 * nh] = -m_rel
        x_ref[1 * nh:2 * nh] = jnp.exp(m_prev - m_rel)
        x_ref[2 * nh:3 * nh] = jnp.exp(jnp.minimum(-(bcum + m_rel), ML_MAX_EXP_ARG))
        x_ref[3 * nh:4 * nh] = jnp.exp(b_last + u - m_new)
        bcast = lax.dot_general(_split3(x_ref[...]), sel_ref[...], TN_DIMS,
                                preferred_element_type=F32)

        def col_tile(vec, h):
            j = (vec * nh + h) * LANES
            return bcast[:, j:j + LANES]

        v_aug = [jnp.concatenate([v_ref[0, pl.ds(t0, clen), lanes[h]], ones_blk], axis=1)
                 for h in heads]
        sc = [qk[h] * jnp.exp(jnp.where(causal, u[h:h + 1, :] + col_tile(0, h), -jnp.inf))
              for h in heads]
        intra = [jnp.dot(sc[h].astype(BF16), v_aug[h], preferred_element_type=F32) for h in heads]
        h_out = []
        for h in heads:
            g_inter = col_tile(1, h)
            num = intra[h][:, :dh] + g_inter * inter[h][:, :dh]
            den = intra[h][:, dh:] + g_inter * inter[h][:, dh:]
            h_out.append(num / jnp.maximum(jnp.abs(den), col_tile(2, h)))
        ms = [jnp.dot((h_out[h] * h_out[h]).astype(BF16), mean_mat, preferred_element_type=F32)
              for h in heads]
        kw = [(k[h].astype(F32) * col_tile(3, h)).astype(BF16) for h in heads]
        upd = [lax.dot_general(kw[h], v_aug[h], TN_DIMS, preferred_element_type=F32) for h in heads]
        for h in heads:
            hn = h_out[h] * lax.rsqrt(ms[h] + EPS) * gain_ref[:, lanes[h]]
            gate = g_ref[0, pl.ds(t0, clen), lanes[h]].astype(F32)
            o_ref[0, pl.ds(t0, clen), lanes[h]] = (hn * gate).astype(BF16)
        for h in heads:
            decay = jnp.concatenate([g_c[h:h + 1, :]] * (2 * dh // clen), axis=1)
            cn_ref[h] = decay * cn[h] + upd[h]
        return 0

    lax.fori_loop(0, ts // clen, chunk, 0)


def _mlstm(q, k, v, g, gates, gain):
    bsz, seq, width = q.shape
    dh = width // ML_HEADS
    ts = min(ML_TILE, seq)
    nvec = 4 * ML_HEADS
    sel = (jnp.arange(4 * nvec)[:, None] % nvec == jnp.arange(nvec * LANES)[None, :] // LANES)
    spec = pl.BlockSpec((1, ts, width), lambda b, s: (b, s, 0))
    return pl.pallas_call(
        _ml_kernel,
        grid=(bsz, seq // ts),
        in_specs=[spec, spec, spec, spec,
                  pl.BlockSpec((1, GATE_GROUPS * GATE_ROWS, ts), lambda b, s: (b, 0, s)),
                  pl.BlockSpec((1, width), lambda b, s: (0, 0)),
                  pl.BlockSpec((4 * nvec, nvec * LANES), lambda b, s: (0, 0))],
        out_specs=spec,
        out_shape=jax.ShapeDtypeStruct((bsz, seq, width), BF16),
        scratch_shapes=[pltpu.VMEM((ML_HEADS, dh, 2 * dh), F32),
                        pltpu.VMEM((ML_HEADS, ML_CHUNK), F32),
                        pltpu.VMEM((nvec, ML_CHUNK), F32)],
        compiler_params=pltpu.CompilerParams(dimension_semantics=("parallel", "arbitrary"),
                                             vmem_limit_bytes=VMEM_LIMIT),
        name="mlstm",
    )(q, k, v, g, gates, gain.reshape(1, width), sel.astype(BF16))


def _outproj_kernel(ysb_ref, yml_ref, x_ref, mod_ref, w_ref, o_ref, *, width):
    y = (jnp.dot(ysb_ref[0], w_ref[:width, :], preferred_element_type=F32)
         + jnp.dot(yml_ref[0], w_ref[width:, :], preferred_element_type=F32))
    o_ref[0] = x_ref[0] + mod_ref[2, 0] * y


def _outproj(ysb, yml, x, mod, w_out):
    bsz, seq, d = x.shape
    width = ysb.shape[2]
    ts = min(OUT_TILE, seq)
    ytile = pl.BlockSpec((1, ts, width), lambda b, s: (b, s, 0))
    xtile = pl.BlockSpec((1, ts, d), lambda b, s: (b, s, 0))
    return pl.pallas_call(
        functools.partial(_outproj_kernel, width=width),
        grid=(bsz, seq // ts),
        in_specs=[ytile, ytile, xtile,
                  pl.BlockSpec((3, 1, 1, d), lambda b, s: (0, b, 0, 0)),
                  pl.BlockSpec(w_out.shape, lambda b, s: (0, 0))],
        out_specs=xtile,
        out_shape=jax.ShapeDtypeStruct(x.shape, x.dtype),
        compiler_params=pltpu.CompilerParams(dimension_semantics=("parallel", "parallel"),
                                             vmem_limit_bytes=VMEM_LIMIT),
        name="outproj",
    )(ysb, yml, x, mod.reshape(3, bsz, 1, d), w_out)


def _layer(x, c, w_ada, b_ada, norm_gain, w_in, b_gates, q_norm_gain, k_norm_gain,
           conv_w, conv_b, ml_norm_gain, w_out):
    bsz, seq, d = x.shape
    width = d // 2
    assert d % (2 * LANES) == 0 and width == ML_HEADS * LANES
    assert seq % IN_TILE == 0 and seq % SB_TILE == 0 and seq % ML_TILE == 0
    assert ML_TILE % ML_CHUNK == 0 and IN_TILE % ML_CHUNK == 0 and ML_CHUNK == LANES
    assert seq >= SB_WIDE * SB_TILE
    nmain = 9 * width
    w_main = w_in[:, :nmain].astype(BF16)
    pad = ((0, GATE_ROWS - ML_HEADS), (0, 0))
    wg = w_in[:, nmain:].T
    wg_t = jnp.concatenate([jnp.pad(wg[:ML_HEADS], pad), jnp.pad(wg[ML_HEADS:], pad)]).astype(BF16)
    bg = b_gates.reshape(2 * ML_HEADS, 1)
    bg = jnp.concatenate([jnp.pad(bg[:ML_HEADS], pad), jnp.pad(bg[ML_HEADS:], pad)])
    heads = width // SB_HEAD_DIM
    head_id = jnp.arange(width) // SB_HEAD_DIM
    pmat = jnp.where(head_id[:, None] == head_id[None, :], 1.0 / SB_HEAD_DIM, 0.0).astype(BF16)
    qg = jnp.tile(q_norm_gain, heads).reshape(1, width)
    kg = jnp.tile(k_norm_gain, heads).reshape(1, width)
    kk = jnp.arange(SB_TILE)
    tri = -jnp.concatenate([(kk[:, None] > kk[None, :]).astype(BF16),
                            jnp.ones((SB_TILE, SB_TILE), BF16)], axis=1)

    mod = _mod(c, w_ada, b_ada)
    (qsb, ksb, vsb, gsb, qml, kml, vml, gml, gates) = _inproj(
        x, mod, norm_gain, w_main, wg_t, bg, pmat, qg, kg, conv_w, conv_b)
    ysb = _sb_attention(qsb, ksb, vsb, gsb, tri)
    yml = _mlstm(qml, kml, vml, gml, _gate_scan(gates), ml_norm_gain)
    return _outproj(ysb, yml, x, mod, w_out.astype(BF16))


def kernel(x, c, w_ada, b_ada, norm_gain, w_in, b_gates, q_norm_gain, k_norm_gain, conv_w,
           conv_b, ml_norm_gain, w_out):
    h = x
    for layer in range(w_in.shape[0]):
        h = _layer(h, c, w_ada[layer], b_ada[layer], norm_gain[layer], w_in[layer],
                   b_gates[layer], q_norm_gain[layer], k_norm_gain[layer], conv_w[layer],
                   conv_b[layer], ml_norm_gain[layer], w_out[layer])
    return h
```

```python
import functools
import math

import jax
import jax.numpy as jnp
from jax import lax
from jax.experimental import pallas as pl
from jax.experimental.pallas import tpu as pltpu

F32 = jnp.float32
BF16 = jnp.bfloat16

SB_HEAD_DIM = 64
ML_HEADS = 4
CONV_WIDTH = 4
EPS = 1e-6

LANES = 128
SUBLANES = 8
GATE_ROWS = SUBLANES
GATE_GROUPS = 5
MXU_DIM = 256
GATE_SCAN_SPAN = MXU_DIM
VMEM_LIMIT = 56 * 1024 * 1024

IN_TILE = 512
OUT_TILE = 1024
SB_TILE = 128
SB_WIDE = 3
SB_UNROLL = 3
ML_CHUNK = 128
ML_TILE = 1024
ML_MAX_EXP_ARG = 88.0
SB_LOG2_STICK_FLOOR = -150.0

NT_DIMS = (((1,), (1,)), ((), ()))
TN_DIMS = (((0,), (0,)), ((), ()))


def _sigmoid(u):
    return 0.5 + 0.5 * jnp.tanh(0.5 * u)


def _silu(u):
    h = 0.5 * u
    return h + h * jnp.tanh(h)


def _log_sigmoid(u):
    return jnp.minimum(u, 0.0) - jnp.log1p(jnp.exp(-jnp.abs(u)))


def _mod_kernel(c_ref, w_ref, b_ref, o_ref):
    c = c_ref[...]
    o_ref[0] = jnp.dot(_silu(c), w_ref[...], preferred_element_type=F32) + b_ref[0]


def _mod(c, w_ada, b_ada):
    bsz, d = c.shape
    return pl.pallas_call(
        _mod_kernel,
        grid=(3,),
        in_specs=[pl.BlockSpec((bsz, d), lambda j: (0, 0)),
                  pl.BlockSpec((d, d), lambda j: (0, j)),
                  pl.BlockSpec((1, 1, d), lambda j: (j, 0, 0))],
        out_specs=pl.BlockSpec((1, bsz, d), lambda j: (j, 0, 0)),
        out_shape=jax.ShapeDtypeStruct((3, bsz, d), F32),
        compiler_params=pltpu.CompilerParams(dimension_semantics=("arbitrary",),
                                             vmem_limit_bytes=VMEM_LIMIT),
        name="adaln_mod",
    )(c, w_ada, b_ada.reshape(3, 1, d))


def _split3(x):
    p0 = x.astype(BF16).astype(F32)
    r1 = x - p0
    p1 = r1.astype(BF16).astype(F32)
    return jnp.concatenate([p0, p1, r1 - p1, jnp.zeros_like(x)], axis=0).astype(BF16)


def _exact_rows_dot(x, mat):
    r = x.shape[0]
    y = jnp.dot(_split3(x), mat, preferred_element_type=F32)
    return y[0:r] + y[r:2 * r] + y[2 * r:3 * r]


def _inproj_kernel(x_ref, mod_ref, gain_ref, w_ref, wg_ref, bg_ref, p_ref, qg_ref, kg_ref,
                   cw_ref, cb_ref,
                   qsb_ref, ksb_ref, vsb_ref, gsb_ref, qml_ref, kml_ref, vml_ref, gml_ref,
                   gates_ref, tail_ref, *, width, ml_scale):
    ts = x_ref.shape[1]

    @pl.when(pl.program_id(1) == 0)
    def _():
        tail_ref[...] = jnp.zeros(tail_ref.shape, F32)

    x = x_ref[0]
    ms = jnp.mean(x * x, axis=-1, keepdims=True)
    shift = mod_ref[0, 0]
    scale = mod_ref[1, 0]
    hn = ((x * lax.rsqrt(ms + EPS)) * (gain_ref[...] * (1.0 + scale)) + shift).astype(BF16)

    gates_ref[0] = (lax.dot_general(wg_ref[...], hn, NT_DIMS, preferred_element_type=F32)
                    + bg_ref[...])

    def proj(g):
        return jnp.dot(hn, w_ref[:, g * width:(g + 1) * width], preferred_element_type=F32)


    head_rows = lax.broadcasted_iota(jnp.int32, (SUBLANES, width), 0)
    for idx, (g, out_ref, sc) in enumerate(((4, qml_ref, ml_scale), (5, kml_ref, 1.0))):
        gcols = slice(idx * width, (idx + 1) * width)
        u = proj(g)
        tail = tail_ref[idx]
        acc = cb_ref[:, gcols] + cw_ref[CONV_WIDTH - 1:CONV_WIDTH, gcols] * u
        for k in range(1, CONV_WIDTH):
            delayed = pltpu.roll(u, k, 0)
            head = jnp.where(head_rows < k, pltpu.roll(tail, k, 0), delayed[:SUBLANES])
            delayed = jnp.concatenate([head, delayed[SUBLANES:]], axis=0)
            acc = acc + cw_ref[CONV_WIDTH - 1 - k:CONV_WIDTH - k, gcols] * delayed
        out_ref[0] = (_silu(acc) * sc).astype(BF16)
        tail_ref[idx] = u[ts - SUBLANES:]

    gml_ref[0] = (_sigmoid(proj(7)) * _silu(proj(8))).astype(BF16)
    gsb_ref[0] = _silu(proj(3)).astype(BF16)

    def head_mean_sq(v):
        sq = (v * v).astype(BF16)
        span = p_ref.shape[0]
        return jnp.concatenate([jnp.dot(sq[:, j:j + span], p_ref[...], preferred_element_type=F32)
                                for j in range(0, width, span)], axis=1)

    for g, gn_ref, out_ref, sc in ((0, qg_ref, qsb_ref, math.log2(math.e) / math.sqrt(SB_HEAD_DIM)),
                                   (1, kg_ref, ksb_ref, 1.0)):
        u = proj(g)
        out_ref[0] = (u * lax.rsqrt(head_mean_sq(u) + EPS) * (gn_ref[...] * sc)).astype(BF16)
    vsb_ref[0] = proj(2).astype(BF16)
    vml_ref[0] = proj(6).astype(BF16)


def _inproj(x, mod, norm_gain, w_main, wg_t, b_gates, pmat, qg, kg, conv_w, conv_b):
    bsz, seq, d = x.shape
    width = d // 2
    ts = min(IN_TILE, seq)
    ngroups = w_main.shape[1] // width
    const2 = lambda b, s: (0, 0)
    tile_spec = pl.BlockSpec((1, ts, width), lambda b, s: (b, s, 0))
    act = jax.ShapeDtypeStruct((bsz, seq, width), BF16)
    kern = functools.partial(_inproj_kernel, width=width,
                             ml_scale=1.0 / math.sqrt(width // ML_HEADS))
    return pl.pallas_call(
        kern,
        grid=(bsz, seq // ts),
        in_specs=[pl.BlockSpec((1, ts, d), lambda b, s: (b, s, 0)),
                  pl.BlockSpec((3, 1, 1, d), lambda b, s: (0, b, 0, 0)),
                  pl.BlockSpec((1, d), const2),
                  pl.BlockSpec((d, ngroups * width), const2),
                  pl.BlockSpec((2 * GATE_ROWS, d), const2),
                  pl.BlockSpec((2 * GATE_ROWS, 1), const2),
                  pl.BlockSpec(pmat.shape, const2),
                  pl.BlockSpec((1, width), const2),
                  pl.BlockSpec((1, width), const2),
                  pl.BlockSpec((CONV_WIDTH, 2 * width), const2),
                  pl.BlockSpec((1, 2 * width), const2)],
        out_specs=[tile_spec] * 8 + [pl.BlockSpec((1, 2 * GATE_ROWS, ts), lambda b, s: (b, 0, s))],
        out_shape=[act] * 8 + [jax.ShapeDtypeStruct((bsz, 2 * GATE_ROWS, seq), F32)],
        scratch_shapes=[pltpu.VMEM((2, SUBLANES, width), F32)],
        compiler_params=pltpu.CompilerParams(dimension_semantics=("parallel", "arbitrary"),
                                             vmem_limit_bytes=VMEM_LIMIT),
        name="inproj",
    )(x, mod.reshape(3, bsz, 1, d), norm_gain.reshape(1, d), w_main, wg_t,
      b_gates, pmat, qg, kg, conv_w, conv_b.reshape(1, 2 * width))


def _gate_scan_kernel(gt_ref, scan_ref, o_ref):
    seq = gt_ref.shape[2]
    span = scan_ref.shape[1]

    def per_span(x, mat):
        return jnp.concatenate([_exact_rows_dot(x[:, j:j + span], mat)
                                for j in range(0, seq, span)], axis=1)

    gt = gt_ref[0]
    bcum = per_span(_log_sigmoid(gt[GATE_ROWS:]), scan_ref[0])
    u = gt[:GATE_ROWS] - bcum
    lane_in_chunk = lax.broadcasted_iota(jnp.int32, (GATE_ROWS, seq), 1) % ML_CHUNK
    u_max = u
    step = 1
    while step < ML_CHUNK:
        shifted = jnp.where(lane_in_chunk >= step, pltpu.roll(u_max, step, 1), -jnp.inf)
        u_max = jnp.maximum(u_max, shifted)
        step *= 2
    o_ref[0, 0:GATE_ROWS] = u
    o_ref[0, GATE_ROWS:2 * GATE_ROWS] = bcum
    o_ref[0, 2 * GATE_ROWS:3 * GATE_ROWS] = u_max
    o_ref[0, 3 * GATE_ROWS:5 * GATE_ROWS] = per_span(jnp.concatenate([bcum, u_max], axis=0),
                                                     scan_ref[1])


def _gate_scan(gt):
    bsz, rows, seq = gt.shape
    span = min(GATE_SCAN_SPAN, seq)
    t = jnp.arange(span)
    same_chunk = t[:, None] // ML_CHUNK == t[None, :] // ML_CHUNK
    scan_mats = jnp.stack([same_chunk & (t[:, None] <= t[None, :]),
                           same_chunk & (t[:, None] % ML_CHUNK == ML_CHUNK - 1)]).astype(BF16)
    return pl.pallas_call(
        _gate_scan_kernel,
        grid=(bsz,),
        in_specs=[pl.BlockSpec((1, rows, seq), lambda b: (b, 0, 0)),
                  pl.BlockSpec((2, span, span), lambda b: (0, 0, 0))],
        out_specs=pl.BlockSpec((1, GATE_GROUPS * GATE_ROWS, seq), lambda b: (b, 0, 0)),
        out_shape=jax.ShapeDtypeStruct((bsz, GATE_GROUPS * GATE_ROWS, seq), F32),
        compiler_params=pltpu.CompilerParams(dimension_semantics=("parallel",),
                                             vmem_limit_bytes=VMEM_LIMIT),
        name="gate_scan",
    )(gt, scan_mats)


def _sb_kernel(q_ref, k_ref, v_ref, g_ref, tri_ref, o_ref):
    seq = q_ref.shape[1]
    tq = tk = SB_TILE
    wk = SB_WIDE * tk
    lane_a = lax.broadcasted_iota(jnp.int32, (tq, LANES), 1) < SB_HEAD_DIM
    rows = lax.broadcasted_iota(jnp.int32, (2 * tq, tk), 0)
    cols = lax.broadcasted_iota(jnp.int32, (2 * tq, tk), 1)
    rel = cols - jnp.where(rows < tq, rows, rows - tq)
    diag_strict = rel < 0
    ntri = tri_ref[...]

    def softplus(z):
        return jnp.maximum(z, 0.0) + jnp.log2(1.0 + jnp.exp2(-jnp.abs(z)))

    def key_block(z, sp, carry, mask):
        sp_in = sp if mask is None else jnp.where(mask, sp, 0.0)
        r = jnp.dot(sp_in.astype(BF16), ntri, preferred_element_type=F32)
        a = jnp.exp2(z - sp + carry + r[:, :tk])
        if mask is not None:
            a = jnp.where(mask, a, 0.0)
        return a.astype(BF16), carry + r[:, tk:]

    def split_heads(pv):
        return jnp.where(lane_a, pv[:tq], pv[tq:])

    def wide_steps(i0, count, diag_only):
        blocks = range(count)
        qab, first, start, z = [], [], [], []
        for u in blocks:
            i = i0 + u
            q2 = q_ref[0, pl.ds(pl.multiple_of(i * tq, tq), tq), :]
            zero = jnp.zeros_like(q2)
            qab.append(jnp.concatenate([jnp.where(lane_a, q2, zero), jnp.where(lane_a, zero, q2)],
                                       axis=0))
            first.append(i - (SB_WIDE - 1) if diag_only else jnp.maximum(i - (SB_WIDE - 1), 0))
            start.append(pl.multiple_of(first[u] * tk, tk))
            z.append(lax.dot_general(qab[u], k_ref[0, pl.ds(start[u], wk), :], NT_DIMS,
                                     preferred_element_type=F32))
        sp = [softplus(z[u]) for u in blocks]
        carry = [jnp.zeros((2 * tq, tk), F32) for _ in blocks]
        a_parts = [[None] * SB_WIDE for _ in blocks]
        for m in reversed(range(SB_WIDE)):
            sl = slice(m * tk, (m + 1) * tk)
            for u in blocks:
                if diag_only:
                    mask = diag_strict if m == SB_WIDE - 1 else None
                else:
                    mask = rel < ((i0 + u) * tq - start[u] - m * tk)
                a_parts[u][m], carry[u] = key_block(z[u][:, sl], sp[u][:, sl], carry[u], mask)
        acc = [split_heads(jnp.dot(jnp.concatenate(a_parts[u], axis=1),
                                   v_ref[0, pl.ds(start[u], wk), :], preferred_element_type=F32))
               for u in blocks]
        return [(qab[u], first[u], carry[u], acc[u], jnp.max(carry[u])) for u in blocks]

    def finish(i, qab, first, carry, acc, top):
        t0 = pl.multiple_of(i * tq, tq)

        def cond(st):
            j, _, _, top = st
            return jnp.logical_and(j >= 0, top > SB_LOG2_STICK_FLOOR)

        def body(st):
            j, carry, acc, _ = st
            s0 = pl.multiple_of(j * tk, tk)
            z = lax.dot_general(qab, k_ref[0, pl.ds(s0, tk), :], NT_DIMS,
                                preferred_element_type=F32)
            a, carry = key_block(z, softplus(z), carry, None)
            pv = jnp.dot(a, v_ref[0, pl.ds(s0, tk), :], preferred_element_type=F32)
            return j - 1, carry, acc + split_heads(pv), jnp.max(carry)

        _, _, acc, _ = lax.while_loop(cond, body, (first - 1, carry, acc, top))
        gate = g_ref[0, pl.ds(t0, tq), :].astype(F32)
        o_ref[0, pl.ds(t0, tq), :] = (acc * gate).astype(BF16)

    def run(lo, hi, count, diag_only):
        def step(p, _):
            i0 = lo + p * count
            for u, st in enumerate(wide_steps(i0, count, diag_only)):
                finish(i0 + u, *st)
            return 0
        if hi > lo:
            lax.fori_loop(0, (hi - lo) // count, step, 0)

    nq = seq // tq
    n_lead = SB_WIDE - 1 + (nq - (SB_WIDE - 1)) % SB_UNROLL
    run(0, SB_WIDE - 1, 1, False)
    run(SB_WIDE - 1, n_lead, 1, True)
    run(n_lead, nq, SB_UNROLL, True)


def _sb_attention(q, k, v, g, tri):
    bsz, seq, width = q.shape
    spec = pl.BlockSpec((1, seq, LANES), lambda b, h: (b, 0, h))
    return pl.pallas_call(
        _sb_kernel,
        grid=(bsz, width // LANES),
        in_specs=[spec, spec, spec, spec, pl.BlockSpec(tri.shape, lambda b, h: (0, 0))],
        out_specs=spec,
        out_shape=jax.ShapeDtypeStruct((bsz, seq, width), BF16),
        compiler_params=pltpu.CompilerParams(dimension_semantics=("parallel", "parallel"),
                                             vmem_limit_bytes=VMEM_LIMIT),
        name="sb_attention",
    )(q, k, v, g, tri)


def _ml_kernel(q_ref, k_ref, v_ref, g_ref, gates_ref, gain_ref, sel_ref, o_ref,
               cn_ref, m_ref, x_ref):
    ts = q_ref.shape[1]
    nh = ML_HEADS
    dh = q_ref.shape[2] // nh
    clen = ML_CHUNK
    row = lax.broadcasted_iota(jnp.int32, (clen, clen), 0)
    col = lax.broadcasted_iota(jnp.int32, (clen, clen), 1)
    causal = col <= row
    ones_blk = jnp.ones((clen, dh), BF16)
    mean_mat = jnp.full((dh, dh), 1.0 / dh, BF16)

    @pl.when(pl.program_id(1) == 0)
    def _():
        cn_ref[...] = jnp.zeros(cn_ref.shape, F32)
        m_ref[...] = jnp.zeros(m_ref.shape, F32)

    def chunk(c, _):
        t0 = pl.multiple_of(c * clen, clen)
        heads = range(nh)
        lanes = [slice(h * dh, (h + 1) * dh) for h in heads]

        def gate_rows(group):
            return gates_ref[0, group * GATE_ROWS:group * GATE_ROWS + nh, pl.ds(t0, clen)]

        q = [q_ref[0, pl.ds(t0, clen), lanes[h]] for h in heads]
        k = [k_ref[0, pl.ds(t0, clen), lanes[h]] for h in heads]
        cn = [cn_ref[h] for h in heads]
        qk = [lax.dot_general(q[h], k[h], NT_DIMS, preferred_element_type=F32) for h in heads]
        inter = [jnp.dot(q[h], cn[h].astype(BF16), preferred_element_type=F32) for h in heads]

        u, bcum, u_max, b_last, u_top = (gate_rows(g) for g in range(GATE_GROUPS))
        m_prev = m_ref[...]
        m_rel = jnp.maximum(m_prev, u_max)
        m_new = b_last + jnp.maximum(m_prev, u_top)
        g_c = jnp.exp(b_last + m_prev - m_new)
        m_ref[...] = m_new
        x_ref[0 * nh:1 * nh] = -m_rel
        x_ref[1 * nh:2 * nh] = jnp.exp(m_prev - m_rel)
        x_ref[2 * nh:3 * nh] = jnp.exp(jnp.minimum(-(bcum + m_rel), ML_MAX_EXP_ARG))
        x_ref[3 * nh:4 * nh] = jnp.exp(b_last + u - m_new)
        bcast = lax.dot_general(_split3(x_ref[...]), sel_ref[...], TN_DIMS,
                                preferred_element_type=F32)

        def col_tile(vec, h):
            j = (vec * nh + h) * LANES
            return bcast[:, j:j + LANES]

        v_aug = [jnp.concatenate([v_ref[0, pl.ds(t0, clen), lanes[h]], ones_blk], axis=1)
                 for h in heads]
        sc = [qk[h] * jnp.exp(jnp.where(causal, u[h:h + 1, :] + col_tile(0, h), -jnp.inf))
              for h in heads]
        intra = [jnp.dot(sc[h].astype(BF16), v_aug[h], preferred_element_type=F32) for h in heads]
        h_out = []
        for h in heads:
            g_inter = col_tile(1, h)
            num = intra[h][:, :dh] + g_inter * inter[h][:, :dh]
            den = intra[h][:, dh:] + g_inter * inter[h][:, dh:]
            h_out.append(num / jnp.maximum(jnp.abs(den), col_tile(2, h)))
        ms = [jnp.dot((h_out[h] * h_out[h]).astype(BF16), mean_mat, preferred_element_type=F32)
              for h in heads]
        kw = [(k[h].astype(F32) * col_tile(3, h)).astype(BF16) for h in heads]
        upd = [lax.dot_general(kw[h], v_aug[h], TN_DIMS, preferred_element_type=F32) for h in heads]
        for h in heads:
            hn = h_out[h] * lax.rsqrt(ms[h] + EPS) * gain_ref[:, lanes[h]]
            gate = g_ref[0, pl.ds(t0, clen), lanes[h]].astype(F32)
            o_ref[0, pl.ds(t0, clen), lanes[h]] = (hn * gate).astype(BF16)
        for h in heads:
            decay = jnp.concatenate([g_c[h:h + 1, :]] * (2 * dh // clen), axis=1)
            cn_ref[h] = decay * cn[h] + upd[h]
        return 0

    lax.fori_loop(0, ts // clen, chunk, 0)


def _mlstm(q, k, v, g, gates, gain):
    bsz, seq, width = q.shape
    dh = width // ML_HEADS
    ts = min(ML_TILE, seq)
    nvec = 4 * ML_HEADS
    sel = (jnp.arange(4 * nvec)[:, None] % nvec == jnp.arange(nvec * LANES)[None, :] // LANES)
    spec = pl.BlockSpec((1, ts, width), lambda b, s: (b, s, 0))
    return pl.pallas_call(
        _ml_kernel,
        grid=(bsz, seq // ts),
        in_specs=[spec, spec, spec, spec,
                  pl.BlockSpec((1, GATE_GROUPS * GATE_ROWS, ts), lambda b, s: (b, 0, s)),
                  pl.BlockSpec((1, width), lambda b, s: (0, 0)),
                  pl.BlockSpec((4 * nvec, nvec * LANES), lambda b, s: (0, 0))],
        out_specs=spec,
        out_shape=jax.ShapeDtypeStruct((bsz, seq, width), BF16),
        scratch_shapes=[pltpu.VMEM((ML_HEADS, dh, 2 * dh), F32),
                        pltpu.VMEM((ML_HEADS, ML_CHUNK), F32),
                        pltpu.VMEM((nvec, ML_CHUNK), F32)],
        compiler_params=pltpu.CompilerParams(dimension_semantics=("parallel", "arbitrary"),
                                             vmem_limit_bytes=VMEM_LIMIT),
        name="mlstm",
    )(q, k, v, g, gates, gain.reshape(1, width), sel.astype(BF16))


def _outproj_kernel(ysb_ref, yml_ref, x_ref, mod_ref, w_ref, o_ref, *, width):
    y = (jnp.dot(ysb_ref[0], w_ref[:width, :], preferred_element_type=F32)
         + jnp.dot(yml_ref[0], w_ref[width:, :], preferred_element_type=F32))
    o_ref[0] = x_ref[0] + mod_ref[2, 0] * y


def _outproj(ysb, yml, x, mod, w_out):
    bsz, seq, d = x.shape
    width = ysb.shape[2]
    ts = min(OUT_TILE, seq)
    ytile = pl.BlockSpec((1, ts, width), lambda b, s: (b, s, 0))
    xtile = pl.BlockSpec((1, ts, d), lambda b, s: (b, s, 0))
    return pl.pallas_call(
        functools.partial(_outproj_kernel, width=width),
        grid=(bsz, seq // ts),
        in_specs=[ytile, ytile, xtile,
                  pl.BlockSpec((3, 1, 1, d), lambda b, s: (0, b, 0, 0)),
                  pl.BlockSpec(w_out.shape, lambda b, s: (0, 0))],
        out_specs=xtile,
        out_shape=jax.ShapeDtypeStruct(x.shape, x.dtype),
        compiler_params=pltpu.CompilerParams(dimension_semantics=("parallel", "parallel"),
                                             vmem_limit_bytes=VMEM_LIMIT),
        name="outproj",
    )(ysb, yml, x, mod.reshape(3, bsz, 1, d), w_out)


def _layer(x, c, w_ada, b_ada, norm_gain, w_in, b_gates, q_norm_gain, k_norm_gain,
           conv_w, conv_b, ml_norm_gain, w_out):
    bsz, seq, d = x.shape
    width = d // 2
    assert d % (2 * LANES) == 0 and width == ML_HEADS * LANES
    assert seq % IN_TILE == 0 and seq % SB_TILE == 0 and seq % ML_TILE == 0
    assert ML_TILE % ML_CHUNK == 0 and IN_TILE % ML_CHUNK == 0 and ML_CHUNK == LANES
    assert seq >= SB_WIDE * SB_TILE
    nmain = 9 * width
    w_main = w_in[:, :nmain].astype(BF16)
    pad = ((0, GATE_ROWS - ML_HEADS), (0, 0))
    wg = w_in[:, nmain:].T
    wg_t = jnp.concatenate([jnp.pad(wg[:ML_HEADS], pad), jnp.pad(wg[ML_HEADS:], pad)]).astype(BF16)
    bg = b_gates.reshape(2 * ML_HEADS, 1)
    bg = jnp.concatenate([jnp.pad(bg[:ML_HEADS], pad), jnp.pad(bg[ML_HEADS:], pad)])
    heads = width // SB_HEAD_DIM
    head_id = jnp.arange(min(MXU_DIM, width)) // SB_HEAD_DIM
    pmat = jnp.where(head_id[:, None] == head_id[None, :], 1.0 / SB_HEAD_DIM, 0.0).astype(BF16)
    qg = jnp.tile(q_norm_gain, heads).reshape(1, width)
    kg = jnp.tile(k_norm_gain, heads).reshape(1, width)
    kk = jnp.arange(SB_TILE)
    tri = -jnp.concatenate([(kk[:, None] > kk[None, :]).astype(BF16),
                            jnp.ones((SB_TILE, SB_TILE), BF16)], axis=1)

    mod = _mod(c, w_ada, b_ada)
    (qsb, ksb, vsb, gsb, qml, kml, vml, gml, gates) = _inproj(
        x, mod, norm_gain, w_main, wg_t, bg, pmat, qg, kg, conv_w, conv_b)
    ysb = _sb_attention(qsb, ksb, vsb, gsb, tri)
    yml = _mlstm(qml, kml, vml, gml, _gate_scan(gates), ml_norm_gain)
    return _outproj(ysb, yml, x, mod, w_out.astype(BF16))


def kernel(x, c, w_ada, b_ada, norm_gain, w_in, b_gates, q_norm_gain, k_norm_gain, conv_w,
           conv_b, ml_norm_gain, w_out):
    h = x
    for layer in range(w_in.shape[0]):
        h = _layer(h, c, w_ada[layer], b_ada[layer], norm_gain[layer], w_in[layer],
                   b_gates[layer], q_norm_gain[layer], k_norm_gain[layer], conv_w[layer],
                   conv_b[layer], ml_norm_gain[layer], w_out[layer])
    return h
```

```python
import functools
import math

import jax
import jax.numpy as jnp
from jax import lax
from jax.experimental import pallas as pl
from jax.experimental.pallas import tpu as pltpu

F32 = jnp.float32
BF16 = jnp.bfloat16

SB_HEAD_DIM = 64
ML_HEADS = 4
CONV_WIDTH = 4
EPS = 1e-6

LANES = 128
SUBLANES = 8
GATE_ROWS = SUBLANES
GATE_GROUPS = 5
MXU_DIM = 256
GATE_SCAN_SPAN = MXU_DIM
VMEM_LIMIT = 56 * 1024 * 1024

IN_TILE = 1024
OUT_TILE = 1024
SB_TILE = 128
SB_WIDE = 3
SB_UNROLL = 6
ML_CHUNK = 128
ML_TILE = 1024
ML_MAX_EXP_ARG = 88.0
SB_LOG2_STICK_FLOOR = -150.0

NT_DIMS = (((1,), (1,)), ((), ()))
TN_DIMS = (((0,), (0,)), ((), ()))


def _sigmoid(u):
    return 0.5 + 0.5 * jnp.tanh(0.5 * u)


def _silu(u):
    h = 0.5 * u
    return h + h * jnp.tanh(h)


def _log_sigmoid(u):
    return jnp.minimum(u, 0.0) - jnp.log1p(jnp.exp(-jnp.abs(u)))


def _mod_kernel(c_ref, w_ref, b_ref, o_ref):
    c = c_ref[...]
    o_ref[0] = jnp.dot(_silu(c), w_ref[...], preferred_element_type=F32) + b_ref[0]


def _mod(c, w_ada, b_ada):
    bsz, d = c.shape
    return pl.pallas_call(
        _mod_kernel,
        grid=(3,),
        in_specs=[pl.BlockSpec((bsz, d), lambda j: (0, 0)),
                  pl.BlockSpec((d, d), lambda j: (0, j)),
                  pl.BlockSpec((1, 1, d), lambda j: (j, 0, 0))],
        out_specs=pl.BlockSpec((1, bsz, d), lambda j: (j, 0, 0)),
        out_shape=jax.ShapeDtypeStruct((3, bsz, d), F32),
        compiler_params=pltpu.CompilerParams(dimension_semantics=("arbitrary",),
                                             vmem_limit_bytes=VMEM_LIMIT),
        name="adaln_mod",
    )(c, w_ada, b_ada.reshape(3, 1, d))


def _split3(x):
    p0 = x.astype(BF16).astype(F32)
    r1 = x - p0
    p1 = r1.astype(BF16).astype(F32)
    return jnp.concatenate([p0, p1, r1 - p1, jnp.zeros_like(x)], axis=0).astype(BF16)


def _exact_rows_dot(x, mat):
    r = x.shape[0]
    y = jnp.dot(_split3(x), mat, preferred_element_type=F32)
    return y[0:r] + y[r:2 * r] + y[2 * r:3 * r]


def _inproj_kernel(x_ref, mod_ref, gain_ref, w_ref, wg_ref, bg_ref, p_ref, qg_ref, kg_ref,
                   cw_ref, cb_ref,
                   qsb_ref, ksb_ref, vsb_ref, gsb_ref, qml_ref, kml_ref, vml_ref, gml_ref,
                   gates_ref, tail_ref, *, width, ml_scale):
    ts = x_ref.shape[1]

    @pl.when(pl.program_id(1) == 0)
    def _():
        tail_ref[...] = jnp.zeros(tail_ref.shape, F32)

    x = x_ref[0]
    ms = jnp.mean(x * x, axis=-1, keepdims=True)
    shift = mod_ref[0, 0]
    scale = mod_ref[1, 0]
    hn = ((x * lax.rsqrt(ms + EPS)) * (gain_ref[...] * (1.0 + scale)) + shift).astype(BF16)

    gates_ref[0] = (lax.dot_general(wg_ref[...], hn, NT_DIMS, preferred_element_type=F32)
                    + bg_ref[...])

    def proj(g):
        return jnp.dot(hn, w_ref[:, g * width:(g + 1) * width], preferred_element_type=F32)


    head_rows = lax.broadcasted_iota(jnp.int32, (SUBLANES, width), 0)
    for idx, (g, out_ref, sc) in enumerate(((4, qml_ref, ml_scale), (5, kml_ref, 1.0))):
        gcols = slice(idx * width, (idx + 1) * width)
        u = proj(g)
        tail = tail_ref[idx]
        acc = cb_ref[:, gcols] + cw_ref[CONV_WIDTH - 1:CONV_WIDTH, gcols] * u
        for k in range(1, CONV_WIDTH):
            delayed = pltpu.roll(u, k, 0)
            head = jnp.where(head_rows < k, pltpu.roll(tail, k, 0), delayed[:SUBLANES])
            delayed = jnp.concatenate([head, delayed[SUBLANES:]], axis=0)
            acc = acc + cw_ref[CONV_WIDTH - 1 - k:CONV_WIDTH - k, gcols] * delayed
        out_ref[0] = (_silu(acc) * sc).astype(BF16)
        tail_ref[idx] = u[ts - SUBLANES:]

    gml_ref[0] = (_sigmoid(proj(7)) * _silu(proj(8))).astype(BF16)
    gsb_ref[0] = _silu(proj(3)).astype(BF16)

    def head_mean_sq(v):
        sq = (v * v).astype(BF16)
        span = p_ref.shape[0]
        return jnp.concatenate([jnp.dot(sq[:, j:j + span], p_ref[...], preferred_element_type=F32)
                                for j in range(0, width, span)], axis=1)

    for g, gn_ref, out_ref, sc in ((0, qg_ref, qsb_ref, math.log2(math.e) / math.sqrt(SB_HEAD_DIM)),
                                   (1, kg_ref, ksb_ref, 1.0)):
        u = proj(g)
        out_ref[0] = (u * lax.rsqrt(head_mean_sq(u) + EPS) * (gn_ref[...] * sc)).astype(BF16)
    vsb_ref[0] = proj(2).astype(BF16)
    vml_ref[0] = proj(6).astype(BF16)


def _inproj(x, mod, norm_gain, w_main, wg_t, b_gates, pmat, qg, kg, conv_w, conv_b):
    bsz, seq, d = x.shape
    width = d // 2
    ts = min(IN_TILE, seq)
    ngroups = w_main.shape[1] // width
    const2 = lambda b, s: (0, 0)
    tile_spec = pl.BlockSpec((1, ts, width), lambda b, s: (b, s, 0))
    act = jax.ShapeDtypeStruct((bsz, seq, width), BF16)
    kern = functools.partial(_inproj_kernel, width=width,
                             ml_scale=1.0 / math.sqrt(width // ML_HEADS))
    return pl.pallas_call(
        kern,
        grid=(bsz, seq // ts),
        in_specs=[pl.BlockSpec((1, ts, d), lambda b, s: (b, s, 0)),
                  pl.BlockSpec((3, 1, 1, d), lambda b, s: (0, b, 0, 0)),
                  pl.BlockSpec((1, d), const2),
                  pl.BlockSpec((d, ngroups * width), const2, pipeline_mode=pl.Buffered(1)),
                  pl.BlockSpec((2 * GATE_ROWS, d), const2),
                  pl.BlockSpec((2 * GATE_ROWS, 1), const2),
                  pl.BlockSpec(pmat.shape, const2),
                  pl.BlockSpec((1, width), const2),
                  pl.BlockSpec((1, width), const2),
                  pl.BlockSpec((CONV_WIDTH, 2 * width), const2),
                  pl.BlockSpec((1, 2 * width), const2)],
        out_specs=[tile_spec] * 8 + [pl.BlockSpec((1, 2 * GATE_ROWS, ts), lambda b, s: (b, 0, s))],
        out_shape=[act] * 8 + [jax.ShapeDtypeStruct((bsz, 2 * GATE_ROWS, seq), F32)],
        scratch_shapes=[pltpu.VMEM((2, SUBLANES, width), F32)],
        compiler_params=pltpu.CompilerParams(dimension_semantics=("parallel", "arbitrary"),
                                             vmem_limit_bytes=VMEM_LIMIT),
        name="inproj",
    )(x, mod.reshape(3, bsz, 1, d), norm_gain.reshape(1, d), w_main, wg_t,
      b_gates, pmat, qg, kg, conv_w, conv_b.reshape(1, 2 * width))


def _gate_scan_kernel(gt_ref, scan_ref, o_ref):
    seq = gt_ref.shape[2]
    span = scan_ref.shape[1]

    def per_span(x, mat):
        return jnp.concatenate([_exact_rows_dot(x[:, j:j + span], mat)
                                for j in range(0, seq, span)], axis=1)

    gt = gt_ref[0]
    bcum = per_span(_log_sigmoid(gt[GATE_ROWS:]), scan_ref[0])
    u = gt[:GATE_ROWS] - bcum
    lane_in_chunk = lax.broadcasted_iota(jnp.int32, (GATE_ROWS, seq), 1) % ML_CHUNK
    u_max = u
    step = 1
    while step < ML_CHUNK:
        shifted = jnp.where(lane_in_chunk >= step, pltpu.roll(u_max, step, 1), -jnp.inf)
        u_max = jnp.maximum(u_max, shifted)
        step *= 2
    o_ref[0, 0:GATE_ROWS] = u
    o_ref[0, GATE_ROWS:2 * GATE_ROWS] = bcum
    o_ref[0, 2 * GATE_ROWS:3 * GATE_ROWS] = u_max
    o_ref[0, 3 * GATE_ROWS:5 * GATE_ROWS] = per_span(jnp.concatenate([bcum, u_max], axis=0),
                                                     scan_ref[1])


def _gate_scan(gt):
    bsz, rows, seq = gt.shape
    span = min(GATE_SCAN_SPAN, seq)
    t = jnp.arange(span)
    same_chunk = t[:, None] // ML_CHUNK == t[None, :] // ML_CHUNK
    scan_mats = jnp.stack([same_chunk & (t[:, None] <= t[None, :]),
                           same_chunk & (t[:, None] % ML_CHUNK == ML_CHUNK - 1)]).astype(BF16)
    return pl.pallas_call(
        _gate_scan_kernel,
        grid=(bsz,),
        in_specs=[pl.BlockSpec((1, rows, seq), lambda b: (b, 0, 0)),
                  pl.BlockSpec((2, span, span), lambda b: (0, 0, 0))],
        out_specs=pl.BlockSpec((1, GATE_GROUPS * GATE_ROWS, seq), lambda b: (b, 0, 0)),
        out_shape=jax.ShapeDtypeStruct((bsz, GATE_GROUPS * GATE_ROWS, seq), F32),
        compiler_params=pltpu.CompilerParams(dimension_semantics=("parallel",),
                                             vmem_limit_bytes=VMEM_LIMIT),
        name="gate_scan",
    )(gt, scan_mats)


def _sb_kernel(q_ref, k_ref, v_ref, g_ref, tri_ref, o_ref):
    seq = q_ref.shape[1]
    tq = tk = SB_TILE
    wk = SB_WIDE * tk
    lane_a = lax.broadcasted_iota(jnp.int32, (tq, LANES), 1) < SB_HEAD_DIM
    rows = lax.broadcasted_iota(jnp.int32, (2 * tq, tk), 0)
    cols = lax.broadcasted_iota(jnp.int32, (2 * tq, tk), 1)
    rel = cols - jnp.where(rows < tq, rows, rows - tq)
    diag_strict = rel < 0
    ntri = tri_ref[...]

    def softplus(z):
        return jnp.maximum(z, 0.0) + jnp.log2(1.0 + jnp.exp2(-jnp.abs(z)))

    def key_block(z, sp, carry, mask):
        sp_in = sp if mask is None else jnp.where(mask, sp, 0.0)
        r = jnp.dot(sp_in.astype(BF16), ntri, preferred_element_type=F32)
        a = jnp.exp2(z - sp + carry + r[:, :tk])
        if mask is not None:
            a = jnp.where(mask, a, 0.0)
        return a.astype(BF16), carry + r[:, tk:]

    def split_heads(pv):
        return jnp.where(lane_a, pv[:tq], pv[tq:])

    def wide_phases(ids, diag_only):
        blocks = range(len(ids))
        st = {}

        def scores():
            st["qab"], st["first"], st["start"], st["z"] = [], [], [], []
            for i in ids:
                q2 = q_ref[0, pl.ds(pl.multiple_of(i * tq, tq), tq), :]
                zero = jnp.zeros_like(q2)
                qab = jnp.concatenate([jnp.where(lane_a, q2, zero), jnp.where(lane_a, zero, q2)],
                                      axis=0)
                first = i - (SB_WIDE - 1) if diag_only else jnp.maximum(i - (SB_WIDE - 1), 0)
                start = pl.multiple_of(first * tk, tk)
                st["qab"].append(qab)
                st["first"].append(first)
                st["start"].append(start)
                st["z"].append(lax.dot_general(qab, k_ref[0, pl.ds(start, wk), :], NT_DIMS,
                                               preferred_element_type=F32))

        def log_terms():
            st["sp"] = [softplus(z) for z in st["z"]]

        def weights():
            carry = [jnp.zeros((2 * tq, tk), F32) for _ in blocks]
            a_parts = [[None] * SB_WIDE for _ in blocks]
            for m in reversed(range(SB_WIDE)):
                sl = slice(m * tk, (m + 1) * tk)
                for u in blocks:
                    if diag_only:
                        mask = diag_strict if m == SB_WIDE - 1 else None
                    else:
                        mask = rel < (ids[u] * tq - st["start"][u] - m * tk)
                    a_parts[u][m], carry[u] = key_block(st["z"][u][:, sl], st["sp"][u][:, sl],
                                                        carry[u], mask)
            st["a"] = [jnp.concatenate(parts, axis=1) for parts in a_parts]
            st["carry"] = carry

        def values():
            st["out"] = [(st["qab"][u], st["first"][u], st["carry"][u],
                          split_heads(jnp.dot(st["a"][u], v_ref[0, pl.ds(st["start"][u], wk), :],
                                              preferred_element_type=F32)),
                          jnp.max(st["carry"][u])) for u in blocks]

        return [scores, log_terms, weights, values], st

    def wide_steps(i0, count, diag_only):
        ids = [i0 + u for u in range(count)]
        if count < 2:
            phases, st = wide_phases(ids, diag_only)
            for phase in phases:
                phase()
            return st["out"]
        (pa, sta), (pb, stb) = (wide_phases(ids[:count // 2], diag_only),
                                wide_phases(ids[count // 2:], diag_only))
        for phase in (pa[0], pa[1], pb[0], pa[2], pb[1], pa[3], pb[2], pb[3]):
            phase()
        return sta["out"] + stb["out"]

    def finish(i, qab, first, carry, acc, top):
        t0 = pl.multiple_of(i * tq, tq)

        def cond(st):
            j, _, _, top = st
            return jnp.logical_and(j >= 0, top > SB_LOG2_STICK_FLOOR)

        def body(st):
            j, carry, acc, _ = st
            s0 = pl.multiple_of(j * tk, tk)
            z = lax.dot_general(qab, k_ref[0, pl.ds(s0, tk), :], NT_DIMS,
                                preferred_element_type=F32)
            a, carry = key_block(z, softplus(z), carry, None)
            pv = jnp.dot(a, v_ref[0, pl.ds(s0, tk), :], preferred_element_type=F32)
            return j - 1, carry, acc + split_heads(pv), jnp.max(carry)

        _, _, acc, _ = lax.while_loop(cond, body, (first - 1, carry, acc, top))
        gate = g_ref[0, pl.ds(t0, tq), :].astype(F32)
        o_ref[0, pl.ds(t0, tq), :] = (acc * gate).astype(BF16)

    def run(lo, hi, count, diag_only):
        def step(p, _):
            i0 = lo + p * count
            for u, st in enumerate(wide_steps(i0, count, diag_only)):
                finish(i0 + u, *st)
            return 0
        if hi > lo:
            lax.fori_loop(0, (hi - lo) // count, step, 0)

    nq = seq // tq
    n_lead = SB_WIDE - 1 + (nq - (SB_WIDE - 1)) % SB_UNROLL
    run(0, SB_WIDE - 1, 1, False)
    run(SB_WIDE - 1, n_lead, 1, True)
    run(n_lead, nq, SB_UNROLL, True)


def _sb_attention(q, k, v, g, tri):
    bsz, seq, width = q.shape
    spec = pl.BlockSpec((1, seq, LANES), lambda b, h: (b, 0, h))
    return pl.pallas_call(
        _sb_kernel,
        grid=(bsz, width // LANES),
        in_specs=[spec, spec, spec, spec, pl.BlockSpec(tri.shape, lambda b, h: (0, 0))],
        out_specs=spec,
        out_shape=jax.ShapeDtypeStruct((bsz, seq, width), BF16),
        compiler_params=pltpu.CompilerParams(dimension_semantics=("parallel", "parallel"),
                                             vmem_limit_bytes=VMEM_LIMIT),
        name="sb_attention",
    )(q, k, v, g, tri)


def _ml_kernel(q_ref, k_ref, v_ref, g_ref, gates_ref, gain_ref, sel_ref, o_ref,
               cn_ref, m_ref, x_ref):
    ts = q_ref.shape[1]
    nh = ML_HEADS
    dh = q_ref.shape[2] // nh
    clen = ML_CHUNK
    row = lax.broadcasted_iota(jnp.int32, (clen, clen), 0)
    col = lax.broadcasted_iota(jnp.int32, (clen, clen), 1)
    causal = col <= row
    ones_blk = jnp.ones((clen, dh), BF16)
    mean_mat = jnp.full((dh, dh), 1.0 / dh, BF16)

    @pl.when(pl.program_id(1) == 0)
    def _():
        cn_ref[...] = jnp.zeros(cn_ref.shape, F32)
        m_ref[...] = jnp.zeros(m_ref.shape, F32)

    heads = range(nh)
    lanes = [slice(h * dh, (h + 1) * dh) for h in heads]

    def chunk_phases(c, slot):
        t0 = pl.multiple_of(c * clen, clen)
        st = {}

        def gate_rows(group):
            return gates_ref[0, group * GATE_ROWS:group * GATE_ROWS + nh, pl.ds(t0, clen)]

        def col_tile(vec, h):
            j = (vec * nh + h) * LANES
            return st["bcast"][:, j:j + LANES]

        def scores_and_gates():
            st["q"] = [q_ref[0, pl.ds(t0, clen), lanes[h]] for h in heads]
            st["k"] = [k_ref[0, pl.ds(t0, clen), lanes[h]] for h in heads]
            st["qk"] = [lax.dot_general(st["q"][h], st["k"][h], NT_DIMS,
                                        preferred_element_type=F32) for h in heads]
            u, bcum, u_max, b_last, u_top = (gate_rows(g) for g in range(GATE_GROUPS))
            m_prev = m_ref[...]
            m_rel = jnp.maximum(m_prev, u_max)
            m_new = b_last + jnp.maximum(m_prev, u_top)
            m_ref[...] = m_new
            st["u"] = u
            st["g_c"] = jnp.exp(b_last + m_prev - m_new)
            x_ref[slot, 0 * nh:1 * nh] = -m_rel
            x_ref[slot, 1 * nh:2 * nh] = jnp.exp(m_prev - m_rel)
            x_ref[slot, 2 * nh:3 * nh] = jnp.exp(jnp.minimum(-(bcum + m_rel), ML_MAX_EXP_ARG))
            x_ref[slot, 3 * nh:4 * nh] = jnp.exp(b_last + u - m_new)
            st["bcast"] = lax.dot_general(_split3(x_ref[slot]), sel_ref[...], TN_DIMS,
                                          preferred_element_type=F32)

        def outputs():
            st["cn"] = [cn_ref[h] for h in heads]
            inter = [jnp.dot(st["q"][h], st["cn"][h].astype(BF16), preferred_element_type=F32)
                     for h in heads]
            st["v_aug"] = [jnp.concatenate([v_ref[0, pl.ds(t0, clen), lanes[h]], ones_blk], axis=1)
                           for h in heads]
            sc = [st["qk"][h] * jnp.exp(jnp.where(causal, st["u"][h:h + 1, :] + col_tile(0, h),
                                                  -jnp.inf)) for h in heads]
            intra = [jnp.dot(sc[h].astype(BF16), st["v_aug"][h], preferred_element_type=F32)
                     for h in heads]
            st["h_out"] = []
            for h in heads:
                g_inter = col_tile(1, h)
                num = intra[h][:, :dh] + g_inter * inter[h][:, :dh]
                den = intra[h][:, dh:] + g_inter * inter[h][:, dh:]
                st["h_out"].append(num / jnp.maximum(jnp.abs(den), col_tile(2, h)))
            st["ms"] = [jnp.dot((st["h_out"][h] * st["h_out"][h]).astype(BF16), mean_mat,
                                preferred_element_type=F32) for h in heads]

        def state_update():
            kw = [(st["k"][h].astype(F32) * col_tile(3, h)).astype(BF16) for h in heads]
            upd = [lax.dot_general(kw[h], st["v_aug"][h], TN_DIMS, preferred_element_type=F32)
                   for h in heads]
            for h in heads:
                decay = jnp.concatenate([st["g_c"][h:h + 1, :]] * (2 * dh // clen), axis=1)
                cn_ref[h] = decay * st["cn"][h] + upd[h]

        def store():
            for h in heads:
                hn = st["h_out"][h] * lax.rsqrt(st["ms"][h] + EPS) * gain_ref[:, lanes[h]]
                gate = g_ref[0, pl.ds(t0, clen), lanes[h]].astype(F32)
                o_ref[0, pl.ds(t0, clen), lanes[h]] = (hn * gate).astype(BF16)

        return scores_and_gates, outputs, state_update, store

    def chunk_pair(p, _):
        a = chunk_phases(2 * p, 0)
        b = chunk_phases(2 * p + 1, 1)
        for phase in (a[0], b[0], a[1], a[2], b[1], a[3], b[2], b[3]):
            phase()
        return 0

    lax.fori_loop(0, ts // (2 * clen), chunk_pair, 0)


def _mlstm(q, k, v, g, gates, gain):
    bsz, seq, width = q.shape
    dh = width // ML_HEADS
    ts = min(ML_TILE, seq)
    nvec = 4 * ML_HEADS
    sel = (jnp.arange(4 * nvec)[:, None] % nvec == jnp.arange(nvec * LANES)[None, :] // LANES)
    spec = pl.BlockSpec((1, ts, width), lambda b, s: (b, s, 0))
    return pl.pallas_call(
        _ml_kernel,
        grid=(bsz, seq // ts),
        in_specs=[spec, spec, spec, spec,
                  pl.BlockSpec((1, GATE_GROUPS * GATE_ROWS, ts), lambda b, s: (b, 0, s)),
                  pl.BlockSpec((1, width), lambda b, s: (0, 0)),
                  pl.BlockSpec((4 * nvec, nvec * LANES), lambda b, s: (0, 0))],
        out_specs=spec,
        out_shape=jax.ShapeDtypeStruct((bsz, seq, width), BF16),
        scratch_shapes=[pltpu.VMEM((ML_HEADS, dh, 2 * dh), F32),
                        pltpu.VMEM((ML_HEADS, ML_CHUNK), F32),
                        pltpu.VMEM((2, nvec, ML_CHUNK), F32)],
        compiler_params=pltpu.CompilerParams(dimension_semantics=("parallel", "arbitrary"),
                                             vmem_limit_bytes=VMEM_LIMIT),
        name="mlstm",
    )(q, k, v, g, gates, gain.reshape(1, width), sel.astype(BF16))


def _outproj_kernel(ysb_ref, yml_ref, x_ref, mod_ref, w_ref, o_ref, *, width):
    y = (jnp.dot(ysb_ref[0], w_ref[:width, :], preferred_element_type=F32)
         + jnp.dot(yml_ref[0], w_ref[width:, :], preferred_element_type=F32))
    o_ref[0] = x_ref[0] + mod_ref[2, 0] * y


def _outproj(ysb, yml, x, mod, w_out):
    bsz, seq, d = x.shape
    width = ysb.shape[2]
    ts = min(OUT_TILE, seq)
    ytile = pl.BlockSpec((1, ts, width), lambda b, s: (b, s, 0))
    xtile = pl.BlockSpec((1, ts, d), lambda b, s: (b, s, 0))
    return pl.pallas_call(
        functools.partial(_outproj_kernel, width=width),
        grid=(bsz, seq // ts),
        in_specs=[ytile, ytile, xtile,
                  pl.BlockSpec((3, 1, 1, d), lambda b, s: (0, b, 0, 0)),
                  pl.BlockSpec(w_out.shape, lambda b, s: (0, 0))],
        out_specs=xtile,
        out_shape=jax.ShapeDtypeStruct(x.shape, x.dtype),
        compiler_params=pltpu.CompilerParams(dimension_semantics=("parallel", "parallel"),
                                             vmem_limit_bytes=VMEM_LIMIT),
        name="outproj",
    )(ysb, yml, x, mod.reshape(3, bsz, 1, d), w_out)


def _layer(x, c, w_ada, b_ada, norm_gain, w_in, b_gates, q_norm_gain, k_norm_gain,
           conv_w, conv_b, ml_norm_gain, w_out):
    bsz, seq, d = x.shape
    width = d // 2
    assert d % (2 * LANES) == 0 and width == ML_HEADS * LANES
    assert seq % IN_TILE == 0 and seq % SB_TILE == 0 and seq % ML_TILE == 0
    assert ML_TILE % (2 * ML_CHUNK) == 0 and ML_CHUNK == LANES
    assert seq >= SB_WIDE * SB_TILE
    nmain = 9 * width
    w_main = w_in[:, :nmain].astype(BF16)
    pad = ((0, GATE_ROWS - ML_HEADS), (0, 0))
    wg = w_in[:, nmain:].T
    wg_t = jnp.concatenate([jnp.pad(wg[:ML_HEADS], pad), jnp.pad(wg[ML_HEADS:], pad)]).astype(BF16)
    bg = b_gates.reshape(2 * ML_HEADS, 1)
    bg = jnp.concatenate([jnp.pad(bg[:ML_HEADS], pad), jnp.pad(bg[ML_HEADS:], pad)])
    heads = width // SB_HEAD_DIM
    head_id = jnp.arange(min(MXU_DIM, width)) // SB_HEAD_DIM
    pmat = jnp.where(head_id[:, None] == head_id[None, :], 1.0 / SB_HEAD_DIM, 0.0).astype(BF16)
    qg = jnp.tile(q_norm_gain, heads).reshape(1, width)
    kg = jnp.tile(k_norm_gain, heads).reshape(1, width)
    kk = jnp.arange(SB_TILE)
    tri = -jnp.concatenate([(kk[:, None] > kk[None, :]).astype(BF16),
                            jnp.ones((SB_TILE, SB_TILE), BF16)], axis=1)

    mod = _mod(c, w_ada, b_ada)
    (qsb, ksb, vsb, gsb, qml, kml, vml, gml, gates) = _inproj(
        x, mod, norm_gain, w_main, wg_t, bg, pmat, qg, kg, conv_w, conv_b)
    ysb = _sb_attention(qsb, ksb, vsb, gsb, tri)
    yml = _mlstm(qml, kml, vml, gml, _gate_scan(gates), ml_norm_gain)
    return _outproj(ysb, yml, x, mod, w_out.astype(BF16))


def kernel(x, c, w_ada, b_ada, norm_gain, w_in, b_gates, q_norm_gain, k_norm_gain, conv_w,
           conv_b, ml_norm_gain, w_out):
    h = x
    for layer in range(w_in.shape[0]):
        h = _layer(h, c, w_ada[layer], b_ada[layer], norm_gain[layer], w_in[layer],
                   b_gates[layer], q_norm_gain[layer], k_norm_gain[layer], conv_w[layer],
                   conv_b[layer], ml_norm_gain[layer], w_out[layer])
    return h
```

```python
import functools
import math

import jax
import jax.numpy as jnp
from jax import lax
from jax.experimental import pallas as pl
from jax.experimental.pallas import tpu as pltpu

F32 = jnp.float32
BF16 = jnp.bfloat16

SB_HEAD_DIM = 64
ML_HEADS = 4
CONV_WIDTH = 4
EPS = 1e-6

LANES = 128
SUBLANES = 8
GATE_ROWS = SUBLANES
GATE_GROUPS = 5
MXU_DIM = 256
GATE_SCAN_SPAN = MXU_DIM
GATE_SCAN_BATCH = 4
VMEM_LIMIT = 56 * 1024 * 1024

IN_TILE = 1024
OUT_TILE = 1024
SB_TILE = 128
SB_WIDE = 3
SB_UNROLL = 6
ML_CHUNK = 128
ML_TILE = 1024
ML_MAX_EXP_ARG = 88.0
SB_LOG2_STICK_FLOOR = -150.0

NT_DIMS = (((1,), (1,)), ((), ()))
TN_DIMS = (((0,), (0,)), ((), ()))


def _sigmoid(u):
    return 0.5 + 0.5 * jnp.tanh(0.5 * u)


def _silu(u):
    h = 0.5 * u
    return h + h * jnp.tanh(h)


def _log_sigmoid(u):
    return jnp.minimum(u, 0.0) - jnp.log1p(jnp.exp(-jnp.abs(u)))


def _mod_kernel(c_ref, w_ref, b_ref, o_ref):
    c = c_ref[...]
    o_ref[0] = jnp.dot(_silu(c), w_ref[...], preferred_element_type=F32) + b_ref[0]


def _mod(c, w_ada, b_ada):
    bsz, d = c.shape
    return pl.pallas_call(
        _mod_kernel,
        grid=(3,),
        in_specs=[pl.BlockSpec((bsz, d), lambda j: (0, 0)),
                  pl.BlockSpec((d, d), lambda j: (0, j)),
                  pl.BlockSpec((1, 1, d), lambda j: (j, 0, 0))],
        out_specs=pl.BlockSpec((1, bsz, d), lambda j: (j, 0, 0)),
        out_shape=jax.ShapeDtypeStruct((3, bsz, d), F32),
        compiler_params=pltpu.CompilerParams(dimension_semantics=("arbitrary",),
                                             vmem_limit_bytes=VMEM_LIMIT),
        name="adaln_mod",
    )(c, w_ada, b_ada.reshape(3, 1, d))


def _split3(x):
    p0 = x.astype(BF16).astype(F32)
    r1 = x - p0
    p1 = r1.astype(BF16).astype(F32)
    return jnp.concatenate([p0, p1, r1 - p1, jnp.zeros_like(x)], axis=0).astype(BF16)


def _exact_rows_dot(x, mat):
    r = x.shape[0]
    y = jnp.dot(_split3(x), mat, preferred_element_type=F32)
    return y[0:r] + y[r:2 * r] + y[2 * r:3 * r]


def _inproj_kernel(x0_ref, xb_ref, xn_ref, mod_ref, gain_ref, w_ref, wg_ref, bg_ref, p_ref,
                   qg_ref, kg_ref, cw_ref, cb_ref,
                   qsb_ref, ksb_ref, vsb_ref, gsb_ref, qml_ref, kml_ref, vml_ref, gml_ref,
                   gates_ref, tail_ref, hna_ref, *, width, ml_scale):
    half = xb_ref.shape[1]
    head_rows = lax.broadcasted_iota(jnp.int32, (SUBLANES, width), 0)

    def normalise(x):
        ms = jnp.mean(x * x, axis=-1, keepdims=True)
        shift = mod_ref[0, 0]
        scale = mod_ref[1, 0]
        return ((x * lax.rsqrt(ms + EPS)) * (gain_ref[...] * (1.0 + scale)) + shift).astype(BF16)

    @pl.when(pl.program_id(1) == 0)
    def _():
        tail_ref[...] = jnp.zeros(tail_ref.shape, F32)
        hna_ref[...] = normalise(x0_ref[0])

    def project(hn, rows, prefetch):
        gates_ref[0, :, rows] = (lax.dot_general(wg_ref[...], hn, NT_DIMS,
                                                 preferred_element_type=F32) + bg_ref[...])

        def proj(g):
            return jnp.dot(hn, w_ref[:, g * width:(g + 1) * width], preferred_element_type=F32)


        for idx, (g, out_ref, sc) in enumerate(((4, qml_ref, ml_scale), (5, kml_ref, 1.0))):
            gcols = slice(idx * width, (idx + 1) * width)
            u = proj(g)
            tail = tail_ref[idx]
            acc = cb_ref[:, gcols] + cw_ref[CONV_WIDTH - 1:CONV_WIDTH, gcols] * u
            for k in range(1, CONV_WIDTH):
                delayed = pltpu.roll(u, k, 0)
                head = jnp.where(head_rows < k, pltpu.roll(tail, k, 0), delayed[:SUBLANES])
                delayed = jnp.concatenate([head, delayed[SUBLANES:]], axis=0)
                acc = acc + cw_ref[CONV_WIDTH - 1 - k:CONV_WIDTH - k, gcols] * delayed
            out_ref[0, rows] = (_silu(acc) * sc).astype(BF16)
            tail_ref[idx] = u[half - SUBLANES:]

        prefetched = prefetch()
        gml_ref[0, rows] = (_sigmoid(proj(7)) * _silu(proj(8))).astype(BF16)
        gsb_ref[0, rows] = _silu(proj(3)).astype(BF16)

        def head_mean_sq(v):
            sq = (v * v).astype(BF16)
            span = p_ref.shape[0]
            return jnp.concatenate(
                [jnp.dot(sq[:, j:j + span], p_ref[...], preferred_element_type=F32)
                 for j in range(0, width, span)], axis=1)

        for g, gn_ref, out_ref, sc in (
                (0, qg_ref, qsb_ref, math.log2(math.e) / math.sqrt(SB_HEAD_DIM)),
                (1, kg_ref, ksb_ref, 1.0)):
            u = proj(g)
            out_ref[0, rows] = (u * lax.rsqrt(head_mean_sq(u) + EPS)
                                * (gn_ref[...] * sc)).astype(BF16)
        vsb_ref[0, rows] = proj(2).astype(BF16)
        vml_ref[0, rows] = proj(6).astype(BF16)
        return prefetched

    hn_b = project(hna_ref[...], slice(0, half), lambda: normalise(xb_ref[0]))
    hna_ref[...] = project(hn_b, slice(half, 2 * half), lambda: normalise(xn_ref[0]))


def _inproj(x, mod, norm_gain, w_main, wg_t, b_gates, pmat, qg, kg, conv_w, conv_b):
    bsz, seq, d = x.shape
    width = d // 2
    ts = min(IN_TILE, seq)
    half = ts // 2
    last_half = seq // half - 1
    ngroups = 9
    const2 = lambda b, s: (0, 0)
    tile_spec = pl.BlockSpec((1, ts, width), lambda b, s: (b, s, 0))
    act = jax.ShapeDtypeStruct((bsz, seq, width), BF16)
    kern = functools.partial(_inproj_kernel, width=width,
                             ml_scale=1.0 / math.sqrt(width // ML_HEADS))
    return pl.pallas_call(
        kern,
        grid=(bsz, seq // ts),
        in_specs=[pl.BlockSpec((1, half, d), lambda b, s: (b, 0, 0)),
                  pl.BlockSpec((1, half, d), lambda b, s: (b, 2 * s + 1, 0)),
                  pl.BlockSpec((1, half, d), lambda b, s: (b, jnp.minimum(2 * s + 2, last_half), 0)),
                  pl.BlockSpec((3, 1, 1, d), lambda b, s: (0, b, 0, 0)),
                  pl.BlockSpec((1, d), const2),
                  pl.BlockSpec((d, ngroups * width), const2, pipeline_mode=pl.Buffered(1)),
                  pl.BlockSpec((2 * GATE_ROWS, d), const2),
                  pl.BlockSpec((2 * GATE_ROWS, 1), const2),
                  pl.BlockSpec(pmat.shape, const2),
                  pl.BlockSpec((1, width), const2),
                  pl.BlockSpec((1, width), const2),
                  pl.BlockSpec((CONV_WIDTH, 2 * width), const2),
                  pl.BlockSpec((1, 2 * width), const2)],
        out_specs=[tile_spec] * 8 + [pl.BlockSpec((1, 2 * GATE_ROWS, ts), lambda b, s: (b, 0, s))],
        out_shape=[act] * 8 + [jax.ShapeDtypeStruct((bsz, 2 * GATE_ROWS, seq), F32)],
        scratch_shapes=[pltpu.VMEM((2, SUBLANES, width), F32), pltpu.VMEM((half, d), BF16)],
        compiler_params=pltpu.CompilerParams(dimension_semantics=("parallel", "arbitrary"),
                                             vmem_limit_bytes=VMEM_LIMIT),
        name="inproj",
    )(x, x, x, mod.reshape(3, bsz, 1, d), norm_gain.reshape(1, d), w_main, wg_t,
      b_gates, pmat, qg, kg, conv_w, conv_b.reshape(1, 2 * width))


def _gate_scan_kernel(gt_ref, scan_ref, o_ref):
    nb, _, seq = gt_ref.shape
    span = scan_ref.shape[1]
    rows = nb * GATE_ROWS

    def per_span(x, mat):
        return jnp.concatenate([_exact_rows_dot(x[:, j:j + span], mat)
                                for j in range(0, seq, span)], axis=1)

    i_pre = gt_ref[:, :GATE_ROWS, :].reshape(rows, seq)
    f_pre = gt_ref[:, GATE_ROWS:, :].reshape(rows, seq)
    bcum = per_span(_log_sigmoid(f_pre), scan_ref[0])
    u = i_pre - bcum
    lane_in_chunk = lax.broadcasted_iota(jnp.int32, (rows, seq), 1) % ML_CHUNK
    u_max = u
    step = 1
    while step < ML_CHUNK:
        shifted = jnp.where(lane_in_chunk >= step, pltpu.roll(u_max, step, 1), -jnp.inf)
        u_max = jnp.maximum(u_max, shifted)
        step *= 2
    ends = per_span(jnp.concatenate([bcum, u_max], axis=0), scan_ref[1])
    groups = (u, bcum, u_max, ends[:rows], ends[rows:])
    for g, val in enumerate(groups):
        o_ref[:, g * GATE_ROWS:(g + 1) * GATE_ROWS, :] = val.reshape(nb, GATE_ROWS, seq)


def _gate_scan(gt):
    bsz, rows, seq = gt.shape
    span = min(GATE_SCAN_SPAN, seq)
    t = jnp.arange(span)
    same_chunk = t[:, None] // ML_CHUNK == t[None, :] // ML_CHUNK
    scan_mats = jnp.stack([same_chunk & (t[:, None] <= t[None, :]),
                           same_chunk & (t[:, None] % ML_CHUNK == ML_CHUNK - 1)]).astype(BF16)
    nb = math.gcd(bsz, GATE_SCAN_BATCH)
    return pl.pallas_call(
        _gate_scan_kernel,
        grid=(bsz // nb,),
        in_specs=[pl.BlockSpec((nb, rows, seq), lambda b: (b, 0, 0)),
                  pl.BlockSpec((2, span, span), lambda b: (0, 0, 0))],
        out_specs=pl.BlockSpec((nb, GATE_GROUPS * GATE_ROWS, seq), lambda b: (b, 0, 0)),
        out_shape=jax.ShapeDtypeStruct((bsz, GATE_GROUPS * GATE_ROWS, seq), F32),
        compiler_params=pltpu.CompilerParams(dimension_semantics=("parallel",),
                                             vmem_limit_bytes=VMEM_LIMIT),
        name="gate_scan",
    )(gt, scan_mats)


def _sb_kernel(q_ref, k_ref, v_ref, g_ref, tri_ref, o_ref):
    seq = q_ref.shape[1]
    tq = tk = SB_TILE
    wk = SB_WIDE * tk
    lane_a = lax.broadcasted_iota(jnp.int32, (tq, LANES), 1) < SB_HEAD_DIM
    rows = lax.broadcasted_iota(jnp.int32, (2 * tq, tk), 0)
    cols = lax.broadcasted_iota(jnp.int32, (2 * tq, tk), 1)
    rel = cols - jnp.where(rows < tq, rows, rows - tq)
    diag_strict = rel < 0
    ntri = tri_ref[...]

    def softplus(z):
        return jnp.maximum(z, 0.0) + jnp.log2(1.0 + jnp.exp2(-jnp.abs(z)))

    def key_block(z, sp, carry, mask):
        sp_in = sp if mask is None else jnp.where(mask, sp, 0.0)
        r = jnp.dot(sp_in.astype(BF16), ntri, preferred_element_type=F32)
        a = jnp.exp2(z - sp + carry + r[:, :tk])
        if mask is not None:
            a = jnp.where(mask, a, 0.0)
        return a.astype(BF16), carry + r[:, tk:]

    def split_heads(pv):
        return jnp.where(lane_a, pv[:tq], pv[tq:])

    def wide_phases(ids, diag_only):
        blocks = range(len(ids))
        st = {}

        def scores():
            st["qab"], st["first"], st["start"], st["z"] = [], [], [], []
            for i in ids:
                q2 = q_ref[0, pl.ds(pl.multiple_of(i * tq, tq), tq), :]
                zero = jnp.zeros_like(q2)
                qab = jnp.concatenate([jnp.where(lane_a, q2, zero), jnp.where(lane_a, zero, q2)],
                                      axis=0)
                first = i - (SB_WIDE - 1) if diag_only else jnp.maximum(i - (SB_WIDE - 1), 0)
                start = pl.multiple_of(first * tk, tk)
                st["qab"].append(qab)
                st["first"].append(first)
                st["start"].append(start)
                st["z"].append(lax.dot_general(qab, k_ref[0, pl.ds(start, wk), :], NT_DIMS,
                                               preferred_element_type=F32))

        def log_terms():
            st["sp"] = [softplus(z) for z in st["z"]]

        def weights():
            carry = [jnp.zeros((2 * tq, tk), F32) for _ in blocks]
            a_parts = [[None] * SB_WIDE for _ in blocks]
            for m in reversed(range(SB_WIDE)):
                sl = slice(m * tk, (m + 1) * tk)
                for u in blocks:
                    if diag_only:
                        mask = diag_strict if m == SB_WIDE - 1 else None
                    else:
                        mask = rel < (ids[u] * tq - st["start"][u] - m * tk)
                    a_parts[u][m], carry[u] = key_block(st["z"][u][:, sl], st["sp"][u][:, sl],
                                                        carry[u], mask)
            st["a"] = [jnp.concatenate(parts, axis=1) for parts in a_parts]
            st["carry"] = carry

        def values():
            st["out"] = [(st["qab"][u], st["first"][u], st["carry"][u],
                          split_heads(jnp.dot(st["a"][u], v_ref[0, pl.ds(st["start"][u], wk), :],
                                              preferred_element_type=F32)),
                          jnp.max(st["carry"][u])) for u in blocks]

        return [scores, log_terms, weights, values], st

    def wide_steps(i0, count, diag_only):
        ids = [i0 + u for u in range(count)]
        if count < 2:
            phases, st = wide_phases(ids, diag_only)
            for phase in phases:
                phase()
            return st["out"]
        (pa, sta), (pb, stb) = (wide_phases(ids[:count // 2], diag_only),
                                wide_phases(ids[count // 2:], diag_only))
        for phase in (pa[0], pa[1], pb[0], pa[2], pb[1], pa[3], pb[2], pb[3]):
            phase()
        return sta["out"] + stb["out"]

    def finish(i, qab, first, carry, acc, top):
        t0 = pl.multiple_of(i * tq, tq)

        def cond(st):
            j, _, _, top = st
            return jnp.logical_and(j >= 0, top > SB_LOG2_STICK_FLOOR)

        def body(st):
            j, carry, acc, _ = st
            s0 = pl.multiple_of(j * tk, tk)
            z = lax.dot_general(qab, k_ref[0, pl.ds(s0, tk), :], NT_DIMS,
                                preferred_element_type=F32)
            a, carry = key_block(z, softplus(z), carry, None)
            pv = jnp.dot(a, v_ref[0, pl.ds(s0, tk), :], preferred_element_type=F32)
            return j - 1, carry, acc + split_heads(pv), jnp.max(carry)

        _, _, acc, _ = lax.while_loop(cond, body, (first - 1, carry, acc, top))
        gate = g_ref[0, pl.ds(t0, tq), :].astype(F32)
        o_ref[0, pl.ds(t0, tq), :] = (acc * gate).astype(BF16)

    def run(lo, hi, count, diag_only):
        def step(p, _):
            i0 = lo + p * count
            for u, st in enumerate(wide_steps(i0, count, diag_only)):
                finish(i0 + u, *st)
            return 0
        if hi > lo:
            lax.fori_loop(0, (hi - lo) // count, step, 0)

    nq = seq // tq
    n_lead = SB_WIDE - 1 + (nq - (SB_WIDE - 1)) % SB_UNROLL
    run(0, SB_WIDE - 1, 1, False)
    run(SB_WIDE - 1, n_lead, 1, True)
    run(n_lead, nq, SB_UNROLL, True)


def _sb_attention(q, k, v, g, tri):
    bsz, seq, width = q.shape
    spec = pl.BlockSpec((1, seq, LANES), lambda b, h: (b, 0, h))
    return pl.pallas_call(
        _sb_kernel,
        grid=(bsz, width // LANES),
        in_specs=[spec, spec, spec, spec, pl.BlockSpec(tri.shape, lambda b, h: (0, 0))],
        out_specs=spec,
        out_shape=jax.ShapeDtypeStruct((bsz, seq, width), BF16),
        compiler_params=pltpu.CompilerParams(dimension_semantics=("parallel", "parallel"),
                                             vmem_limit_bytes=VMEM_LIMIT),
        name="sb_attention",
    )(q, k, v, g, tri)


def _ml_kernel(q_ref, k_ref, v_ref, g_ref, gates_ref, gain_ref, sel_ref, o_ref,
               cn_ref, m_ref, x_ref):
    ts = q_ref.shape[1]
    nh = ML_HEADS
    dh = q_ref.shape[2] // nh
    clen = ML_CHUNK
    row = lax.broadcasted_iota(jnp.int32, (clen, clen), 0)
    col = lax.broadcasted_iota(jnp.int32, (clen, clen), 1)
    causal = col <= row
    ones_blk = jnp.ones((clen, dh), BF16)
    mean_mat = jnp.full((dh, dh), 1.0 / dh, BF16)

    @pl.when(pl.program_id(1) == 0)
    def _():
        cn_ref[...] = jnp.zeros(cn_ref.shape, F32)
        m_ref[...] = jnp.zeros(m_ref.shape, F32)

    heads = range(nh)
    lanes = [slice(h * dh, (h + 1) * dh) for h in heads]

    def chunk_phases(c, slot):
        t0 = pl.multiple_of(c * clen, clen)
        st = {}

        def gate_rows(group):
            return gates_ref[0, group * GATE_ROWS:group * GATE_ROWS + nh, pl.ds(t0, clen)]

        def col_tile(vec, h):
            j = (vec * nh + h) * LANES
            return st["bcast"][:, j:j + LANES]

        def scores_and_gates():
            st["q"] = [q_ref[0, pl.ds(t0, clen), lanes[h]] for h in heads]
            st["k"] = [k_ref[0, pl.ds(t0, clen), lanes[h]] for h in heads]
            st["qk"] = [lax.dot_general(st["q"][h], st["k"][h], NT_DIMS,
                                        preferred_element_type=F32) for h in heads]
            u, bcum, u_max, b_last, u_top = (gate_rows(g) for g in range(GATE_GROUPS))
            m_prev = m_ref[...]
            m_rel = jnp.maximum(m_prev, u_max)
            m_new = b_last + jnp.maximum(m_prev, u_top)
            m_ref[...] = m_new
            st["u"] = u
            st["g_c"] = jnp.exp(b_last + m_prev - m_new)
            x_ref[slot, 0 * nh:1 * nh] = -m_rel
            x_ref[slot, 1 * nh:2 * nh] = jnp.exp(m_prev - m_rel)
            x_ref[slot, 2 * nh:3 * nh] = jnp.exp(jnp.minimum(-(bcum + m_rel), ML_MAX_EXP_ARG))
            x_ref[slot, 3 * nh:4 * nh] = jnp.exp(b_last + u - m_new)
            st["bcast"] = lax.dot_general(_split3(x_ref[slot]), sel_ref[...], TN_DIMS,
                                          preferred_element_type=F32)

        def outputs():
            st["cn"] = [cn_ref[h] for h in heads]
            inter = [jnp.dot(st["q"][h], st["cn"][h].astype(BF16), preferred_element_type=F32)
                     for h in heads]
            st["v_aug"] = [jnp.concatenate([v_ref[0, pl.ds(t0, clen), lanes[h]], ones_blk], axis=1)
                           for h in heads]
            sc = [st["qk"][h] * jnp.exp(jnp.where(causal, st["u"][h:h + 1, :] + col_tile(0, h),
                                                  -jnp.inf)) for h in heads]
            intra = [jnp.dot(sc[h].astype(BF16), st["v_aug"][h], preferred_element_type=F32)
                     for h in heads]
            st["h_out"] = []
            for h in heads:
                g_inter = col_tile(1, h)
                num = intra[h][:, :dh] + g_inter * inter[h][:, :dh]
                den = intra[h][:, dh:] + g_inter * inter[h][:, dh:]
                st["h_out"].append(num / jnp.maximum(jnp.abs(den), col_tile(2, h)))
            st["ms"] = [jnp.dot((st["h_out"][h] * st["h_out"][h]).astype(BF16), mean_mat,
                                preferred_element_type=F32) for h in heads]

        def state_update():
            kw = [(st["k"][h].astype(F32) * col_tile(3, h)).astype(BF16) for h in heads]
            upd = [lax.dot_general(kw[h], st["v_aug"][h], TN_DIMS, preferred_element_type=F32)
                   for h in heads]
            for h in heads:
                decay = jnp.concatenate([st["g_c"][h:h + 1, :]] * (2 * dh // clen), axis=1)
                cn_ref[h] = decay * st["cn"][h] + upd[h]

        def store():
            for h in heads:
                hn = st["h_out"][h] * lax.rsqrt(st["ms"][h] + EPS) * gain_ref[:, lanes[h]]
                gate = g_ref[0, pl.ds(t0, clen), lanes[h]].astype(F32)
                o_ref[0, pl.ds(t0, clen), lanes[h]] = (hn * gate).astype(BF16)

        return scores_and_gates, outputs, state_update, store

    def chunk_pair(p, _):
        a = chunk_phases(2 * p, 0)
        b = chunk_phases(2 * p + 1, 1)
        for phase in (a[0], b[0], a[1], a[2], b[1], a[3], b[2], b[3]):
            phase()
        return 0

    lax.fori_loop(0, ts // (2 * clen), chunk_pair, 0)


def _mlstm(q, k, v, g, gates, gain):
    bsz, seq, width = q.shape
    dh = width // ML_HEADS
    ts = min(ML_TILE, seq)
    nvec = 4 * ML_HEADS
    sel = (jnp.arange(4 * nvec)[:, None] % nvec == jnp.arange(nvec * LANES)[None, :] // LANES)
    spec = pl.BlockSpec((1, ts, width), lambda b, s: (b, s, 0))
    return pl.pallas_call(
        _ml_kernel,
        grid=(bsz, seq // ts),
        in_specs=[spec, spec, spec, spec,
                  pl.BlockSpec((1, GATE_GROUPS * GATE_ROWS, ts), lambda b, s: (b, 0, s)),
                  pl.BlockSpec((1, width), lambda b, s: (0, 0)),
                  pl.BlockSpec((4 * nvec, nvec * LANES), lambda b, s: (0, 0))],
        out_specs=spec,
        out_shape=jax.ShapeDtypeStruct((bsz, seq, width), BF16),
        scratch_shapes=[pltpu.VMEM((ML_HEADS, dh, 2 * dh), F32),
                        pltpu.VMEM((ML_HEADS, ML_CHUNK), F32),
                        pltpu.VMEM((2, nvec, ML_CHUNK), F32)],
        compiler_params=pltpu.CompilerParams(dimension_semantics=("parallel", "arbitrary"),
                                             vmem_limit_bytes=VMEM_LIMIT),
        name="mlstm",
    )(q, k, v, g, gates, gain.reshape(1, width), sel.astype(BF16))


def _outproj_kernel(ysb_ref, yml_ref, x_ref, mod_ref, w_ref, o_ref, *, width):
    y = (jnp.dot(ysb_ref[0], w_ref[:width, :], preferred_element_type=F32)
         + jnp.dot(yml_ref[0], w_ref[width:, :], preferred_element_type=F32))
    o_ref[0] = x_ref[0] + mod_ref[2, 0] * y


def _outproj(ysb, yml, x, mod, w_out):
    bsz, seq, d = x.shape
    width = ysb.shape[2]
    ts = min(OUT_TILE, seq)
    ytile = pl.BlockSpec((1, ts, width), lambda b, s: (b, s, 0))
    xtile = pl.BlockSpec((1, ts, d), lambda b, s: (b, s, 0))
    return pl.pallas_call(
        functools.partial(_outproj_kernel, width=width),
        grid=(bsz, seq // ts),
        in_specs=[ytile, ytile, xtile,
                  pl.BlockSpec((3, 1, 1, d), lambda b, s: (0, b, 0, 0)),
                  pl.BlockSpec(w_out.shape, lambda b, s: (0, 0))],
        out_specs=xtile,
        out_shape=jax.ShapeDtypeStruct(x.shape, x.dtype),
        compiler_params=pltpu.CompilerParams(dimension_semantics=("parallel", "parallel"),
                                             vmem_limit_bytes=VMEM_LIMIT),
        name="outproj",
    )(ysb, yml, x, mod.reshape(3, bsz, 1, d), w_out)


def _layer(x, c, w_ada, b_ada, norm_gain, w_in, b_gates, q_norm_gain, k_norm_gain,
           conv_w, conv_b, ml_norm_gain, w_out):
    bsz, seq, d = x.shape
    width = d // 2
    assert d % (2 * LANES) == 0 and width == ML_HEADS * LANES
    assert seq % IN_TILE == 0 and seq % SB_TILE == 0 and seq % ML_TILE == 0
    assert ML_TILE % (2 * ML_CHUNK) == 0 and ML_CHUNK == LANES
    assert seq >= SB_WIDE * SB_TILE
    nmain = 9 * width
    w_main = w_in.astype(BF16)
    pad = ((0, GATE_ROWS - ML_HEADS), (0, 0))
    wg = w_in[:, nmain:].T
    wg_t = jnp.concatenate([jnp.pad(wg[:ML_HEADS], pad), jnp.pad(wg[ML_HEADS:], pad)]).astype(BF16)
    bg = b_gates.reshape(2 * ML_HEADS, 1)
    bg = jnp.concatenate([jnp.pad(bg[:ML_HEADS], pad), jnp.pad(bg[ML_HEADS:], pad)])
    heads = width // SB_HEAD_DIM
    head_id = jnp.arange(min(MXU_DIM, width)) // SB_HEAD_DIM
    pmat = jnp.where(head_id[:, None] == head_id[None, :], 1.0 / SB_HEAD_DIM, 0.0).astype(BF16)
    qg = jnp.tile(q_norm_gain, heads).reshape(1, width)
    kg = jnp.tile(k_norm_gain, heads).reshape(1, width)
    kk = jnp.arange(SB_TILE)
    tri = -jnp.concatenate([(kk[:, None] > kk[None, :]).astype(BF16),
                            jnp.ones((SB_TILE, SB_TILE), BF16)], axis=1)

    mod = _mod(c, w_ada, b_ada)
    (qsb, ksb, vsb, gsb, qml, kml, vml, gml, gates) = _inproj(
        x, mod, norm_gain, w_main, wg_t, bg, pmat, qg, kg, conv_w, conv_b)
    ysb = _sb_attention(qsb, ksb, vsb, gsb, tri)
    yml = _mlstm(qml, kml, vml, gml, _gate_scan(gates), ml_norm_gain)
    return _outproj(ysb, yml, x, mod, w_out.astype(BF16))


def kernel(x, c, w_ada, b_ada, norm_gain, w_in, b_gates, q_norm_gain, k_norm_gain, conv_w,
           conv_b, ml_norm_gain, w_out):
    h = x
    for layer in range(w_in.shape[0]):
        h = _layer(h, c, w_ada[layer], b_ada[layer], norm_gain[layer], w_in[layer],
                   b_gates[layer], q_norm_gain[layer], k_norm_gain[layer], conv_w[layer],
                   conv_b[layer], ml_norm_gain[layer], w_out[layer])
    return h
```

```python
import functools
import math

import jax
import jax.numpy as jnp
from jax import lax
from jax.experimental import pallas as pl
from jax.experimental.pallas import tpu as pltpu

F32 = jnp.float32
BF16 = jnp.bfloat16

SB_HEAD_DIM = 64
ML_HEADS = 4
CONV_WIDTH = 4
EPS = 1e-6

LANES = 128
SUBLANES = 8
GATE_ROWS = SUBLANES
GATE_GROUPS = 5
MXU_DIM = 256
GATE_SCAN_SPAN = MXU_DIM
GATE_SCAN_BATCH = 4
VMEM_LIMIT = 56 * 1024 * 1024

IN_TILE = 1024
OUT_TILE = 1024
SB_TILE = 128
SB_WIDE_PARTS = (3, 2)
SB_UNROLL = 6
ML_CHUNK = 128
ML_TILE = 1024
ML_MAX_EXP_ARG = 88.0
SB_LOG2_STICK_FLOOR = -150.0

NT_DIMS = (((1,), (1,)), ((), ()))
TN_DIMS = (((0,), (0,)), ((), ()))


def _sigmoid(u):
    return 0.5 + 0.5 * jnp.tanh(0.5 * u)


def _silu(u):
    h = 0.5 * u
    return h + h * jnp.tanh(h)


def _log_sigmoid(u):
    return jnp.minimum(u, 0.0) - jnp.log1p(jnp.exp(-jnp.abs(u)))


def _mod_kernel(c_ref, w_ref, b_ref, o_ref):
    c = c_ref[...]
    o_ref[0] = jnp.dot(_silu(c), w_ref[...], preferred_element_type=F32) + b_ref[0]


def _mod(c, w_ada, b_ada):
    bsz, d = c.shape
    return pl.pallas_call(
        _mod_kernel,
        grid=(3,),
        in_specs=[pl.BlockSpec((bsz, d), lambda j: (0, 0)),
                  pl.BlockSpec((d, d), lambda j: (0, j)),
                  pl.BlockSpec((1, 1, d), lambda j: (j, 0, 0))],
        out_specs=pl.BlockSpec((1, bsz, d), lambda j: (j, 0, 0)),
        out_shape=jax.ShapeDtypeStruct((3, bsz, d), F32),
        compiler_params=pltpu.CompilerParams(dimension_semantics=("arbitrary",),
                                             vmem_limit_bytes=VMEM_LIMIT),
        name="adaln_mod",
    )(c, w_ada, b_ada.reshape(3, 1, d))


def _split3(x):
    p0 = x.astype(BF16).astype(F32)
    r1 = x - p0
    p1 = r1.astype(BF16).astype(F32)
    return jnp.concatenate([p0, p1, r1 - p1, jnp.zeros_like(x)], axis=0).astype(BF16)


def _exact_rows_dot(x, mat):
    r = x.shape[0]
    y = jnp.dot(_split3(x), mat, preferred_element_type=F32)
    return y[0:r] + y[r:2 * r] + y[2 * r:3 * r]


def _inproj_kernel(x_ref, mod_ref, gain_ref, w_ref, wg_ref, bg_ref, p_ref, qg_ref, kg_ref,
                   cw_ref, cb_ref,
                   qsb_ref, ksb_ref, vsb_ref, gsb_ref, qml_ref, kml_ref, vml_ref, gml_ref,
                   gates_ref, tail_ref, *, width, ml_scale):
    ts = x_ref.shape[1]

    @pl.when(pl.program_id(1) == 0)
    def _():
        tail_ref[...] = jnp.zeros(tail_ref.shape, F32)

    x = x_ref[0]
    ms = jnp.mean(x * x, axis=-1, keepdims=True)
    shift = mod_ref[0, 0]
    scale = mod_ref[1, 0]
    hn = ((x * lax.rsqrt(ms + EPS)) * (gain_ref[...] * (1.0 + scale)) + shift).astype(BF16)

    gates_ref[0] = (lax.dot_general(wg_ref[...], hn, NT_DIMS, preferred_element_type=F32)
                    + bg_ref[...])

    def proj(g):
        return jnp.dot(hn, w_ref[:, g * width:(g + 1) * width], preferred_element_type=F32)


    head_rows = lax.broadcasted_iota(jnp.int32, (SUBLANES, width), 0)
    for idx, (g, out_ref, sc) in enumerate(((4, qml_ref, ml_scale), (5, kml_ref, 1.0))):
        gcols = slice(idx * width, (idx + 1) * width)
        u = proj(g)
        tail = tail_ref[idx]
        acc = cb_ref[:, gcols] + cw_ref[CONV_WIDTH - 1:CONV_WIDTH, gcols] * u
        for k in range(1, CONV_WIDTH):
            delayed = pltpu.roll(u, k, 0)
            head = jnp.where(head_rows < k, pltpu.roll(tail, k, 0), delayed[:SUBLANES])
            delayed = jnp.concatenate([head, delayed[SUBLANES:]], axis=0)
            acc = acc + cw_ref[CONV_WIDTH - 1 - k:CONV_WIDTH - k, gcols] * delayed
        out_ref[0] = (_silu(acc) * sc).astype(BF16)
        tail_ref[idx] = u[ts - SUBLANES:]

    gml_ref[0] = (_sigmoid(proj(7)) * _silu(proj(8))).astype(BF16)
    gsb_ref[0] = _silu(proj(3)).astype(BF16)

    def head_mean_sq(v):
        sq = (v * v).astype(BF16)
        span = p_ref.shape[0]
        return jnp.concatenate([jnp.dot(sq[:, j:j + span], p_ref[...], preferred_element_type=F32)
                                for j in range(0, width, span)], axis=1)

    for g, gn_ref, out_ref, sc in ((0, qg_ref, qsb_ref, math.log2(math.e) / math.sqrt(SB_HEAD_DIM)),
                                   (1, kg_ref, ksb_ref, 1.0)):
        u = proj(g)
        out_ref[0] = (u * lax.rsqrt(head_mean_sq(u) + EPS) * (gn_ref[...] * sc)).astype(BF16)
    vsb_ref[0] = proj(2).astype(BF16)
    vml_ref[0] = proj(6).astype(BF16)


def _inproj(x, mod, norm_gain, w_main, wg_t, b_gates, pmat, qg, kg, conv_w, conv_b):
    bsz, seq, d = x.shape
    width = d // 2
    ts = min(IN_TILE, seq)
    ngroups = 9
    const2 = lambda b, s: (0, 0)
    tile_spec = pl.BlockSpec((1, ts, width), lambda b, s: (b, s, 0))
    act = jax.ShapeDtypeStruct((bsz, seq, width), BF16)
    kern = functools.partial(_inproj_kernel, width=width,
                             ml_scale=1.0 / math.sqrt(width // ML_HEADS))
    return pl.pallas_call(
        kern,
        grid=(bsz, seq // ts),
        in_specs=[pl.BlockSpec((1, ts, d), lambda b, s: (b, s, 0)),
                  pl.BlockSpec((3, 1, 1, d), lambda b, s: (0, b, 0, 0)),
                  pl.BlockSpec((1, d), const2),
                  pl.BlockSpec((d, ngroups * width), const2, pipeline_mode=pl.Buffered(1)),
                  pl.BlockSpec((2 * GATE_ROWS, d), const2),
                  pl.BlockSpec((2 * GATE_ROWS, 1), const2),
                  pl.BlockSpec(pmat.shape, const2),
                  pl.BlockSpec((1, width), const2),
                  pl.BlockSpec((1, width), const2),
                  pl.BlockSpec((CONV_WIDTH, 2 * width), const2),
                  pl.BlockSpec((1, 2 * width), const2)],
        out_specs=[tile_spec] * 8 + [pl.BlockSpec((1, 2 * GATE_ROWS, ts), lambda b, s: (b, 0, s))],
        out_shape=[act] * 8 + [jax.ShapeDtypeStruct((bsz, 2 * GATE_ROWS, seq), F32)],
        scratch_shapes=[pltpu.VMEM((2, SUBLANES, width), F32)],
        compiler_params=pltpu.CompilerParams(dimension_semantics=("parallel", "arbitrary"),
                                             vmem_limit_bytes=VMEM_LIMIT),
        name="inproj",
    )(x, mod.reshape(3, bsz, 1, d), norm_gain.reshape(1, d), w_main, wg_t,
      b_gates, pmat, qg, kg, conv_w, conv_b.reshape(1, 2 * width))


def _gate_scan_kernel(gt_ref, scan_ref, o_ref):
    nb, _, seq = gt_ref.shape
    span = scan_ref.shape[1]
    rows = nb * GATE_ROWS

    def per_span(x, mat):
        return jnp.concatenate([_exact_rows_dot(x[:, j:j + span], mat)
                                for j in range(0, seq, span)], axis=1)

    i_pre = gt_ref[:, :GATE_ROWS, :].reshape(rows, seq)
    f_pre = gt_ref[:, GATE_ROWS:, :].reshape(rows, seq)
    bcum = per_span(_log_sigmoid(f_pre), scan_ref[0])
    u = i_pre - bcum
    lane_in_chunk = lax.broadcasted_iota(jnp.int32, (rows, seq), 1) % ML_CHUNK
    u_max = u
    step = 1
    while step < ML_CHUNK:
        shifted = jnp.where(lane_in_chunk >= step, pltpu.roll(u_max, step, 1), -jnp.inf)
        u_max = jnp.maximum(u_max, shifted)
        step *= 2
    ends = per_span(jnp.concatenate([bcum, u_max], axis=0), scan_ref[1])
    groups = (u, bcum, u_max, ends[:rows], ends[rows:])
    for g, val in enumerate(groups):
        o_ref[:, g * GATE_ROWS:(g + 1) * GATE_ROWS, :] = val.reshape(nb, GATE_ROWS, seq)


def _gate_scan(gt):
    bsz, rows, seq = gt.shape
    span = min(GATE_SCAN_SPAN, seq)
    t = jnp.arange(span)
    same_chunk = t[:, None] // ML_CHUNK == t[None, :] // ML_CHUNK
    scan_mats = jnp.stack([same_chunk & (t[:, None] <= t[None, :]),
                           same_chunk & (t[:, None] % ML_CHUNK == ML_CHUNK - 1)]).astype(BF16)
    nb = math.gcd(bsz, GATE_SCAN_BATCH)
    return pl.pallas_call(
        _gate_scan_kernel,
        grid=(bsz // nb,),
        in_specs=[pl.BlockSpec((nb, rows, seq), lambda b: (b, 0, 0)),
                  pl.BlockSpec((2, span, span), lambda b: (0, 0, 0))],
        out_specs=pl.BlockSpec((nb, GATE_GROUPS * GATE_ROWS, seq), lambda b: (b, 0, 0)),
        out_shape=jax.ShapeDtypeStruct((bsz, GATE_GROUPS * GATE_ROWS, seq), F32),
        compiler_params=pltpu.CompilerParams(dimension_semantics=("parallel",),
                                             vmem_limit_bytes=VMEM_LIMIT),
        name="gate_scan",
    )(gt, scan_mats)


def _sb_kernel(q_ref, k_ref, v_ref, g_ref, tri_ref, o_ref):
    seq = q_ref.shape[1]
    tk = SB_TILE
    tq = tk // len(SB_WIDE_PARTS)
    lane_a = lax.broadcasted_iota(jnp.int32, (tq, LANES), 1) < SB_HEAD_DIM
    rows = lax.broadcasted_iota(jnp.int32, (2 * tq, tk), 0)
    cols = lax.broadcasted_iota(jnp.int32, (2 * tq, tk), 1)
    rel = cols - jnp.where(rows < tq, rows, rows - tq)
    ntri = tri_ref[...]

    def softplus(z):
        return jnp.maximum(z, 0.0) + jnp.log2(1.0 + jnp.exp2(-jnp.abs(z)))

    def key_block(z, sp, carry, mask):
        sp_in = sp if mask is None else jnp.where(mask, sp, 0.0)
        r = jnp.dot(sp_in.astype(BF16), ntri, preferred_element_type=F32)
        a = jnp.exp2(z - sp + carry + r[:, :tk])
        if mask is not None:
            a = jnp.where(mask, a, 0.0)
        return a.astype(BF16), carry + r[:, tk:]

    def split_heads(pv):
        return jnp.where(lane_a, pv[:tq], pv[tq:])

    def wide_phases(units, diag_only):
        idx = range(len(units))
        wide = [SB_WIDE_PARTS[part] for _, part in units]
        st = {}

        def scores():
            st["qab"], st["first"], st["start"], st["z"] = [], [], [], []
            for u, (i, part) in enumerate(units):
                t0 = pl.multiple_of(i * tk + part * tq, tq)
                q2 = q_ref[0, pl.ds(t0, tq), :]
                zero = jnp.zeros_like(q2)
                qab = jnp.concatenate([jnp.where(lane_a, q2, zero), jnp.where(lane_a, zero, q2)],
                                      axis=0)
                first = i - (wide[u] - 1) if diag_only else jnp.maximum(i - (wide[u] - 1), 0)
                start = pl.multiple_of(first * tk, tk)
                st["qab"].append(qab)
                st["first"].append(first)
                st["start"].append(start)
                st["z"].append(lax.dot_general(qab, k_ref[0, pl.ds(start, wide[u] * tk), :],
                                               NT_DIMS, preferred_element_type=F32))

        def log_terms():
            st["sp"] = [softplus(z) for z in st["z"]]

        def weights():
            carry = [jnp.zeros((2 * tq, tk), F32) for _ in idx]
            a_parts = [[None] * wide[u] for u in idx]
            for back in range(max(wide)):
                for u, (i, part) in enumerate(units):
                    m = wide[u] - 1 - back
                    if m < 0:
                        continue
                    if diag_only:
                        mask = rel < part * tq if back == 0 else None
                    else:
                        mask = rel < (i * tk + part * tq - st["start"][u] - m * tk)
                    sl = slice(m * tk, (m + 1) * tk)
                    a_parts[u][m], carry[u] = key_block(st["z"][u][:, sl], st["sp"][u][:, sl],
                                                        carry[u], mask)
            st["a"] = [jnp.concatenate(parts, axis=1) for parts in a_parts]
            st["carry"] = carry

        def values():
            st["out"] = [(st["qab"][u], st["first"][u], st["carry"][u],
                          split_heads(jnp.dot(st["a"][u],
                                              v_ref[0, pl.ds(st["start"][u], wide[u] * tk), :],
                                              preferred_element_type=F32)),
                          jnp.max(st["carry"][u])) for u in idx]

        return [scores, log_terms, weights, values], st

    def wide_steps(units, diag_only):
        if len(units) <= len(SB_WIDE_PARTS):
            phases, st = wide_phases(units, diag_only)
            for phase in phases:
                phase()
            return st["out"]
        (pa, sta), (pb, stb) = (wide_phases(units[:len(units) // 2], diag_only),
                                wide_phases(units[len(units) // 2:], diag_only))
        for phase in (pa[0], pa[1], pb[0], pa[2], pb[1], pa[3], pb[2], pb[3]):
            phase()
        return sta["out"] + stb["out"]

    def finish(unit, qab, first, carry, acc, top):
        t0 = pl.multiple_of(unit[0] * tk + unit[1] * tq, tq)

        def cond(st):
            j, _, _, top = st
            return jnp.logical_and(j >= 0, top > SB_LOG2_STICK_FLOOR)

        def body(st):
            j, carry, acc, _ = st
            s0 = pl.multiple_of(j * tk, tk)
            z = lax.dot_general(qab, k_ref[0, pl.ds(s0, tk), :], NT_DIMS,
                                preferred_element_type=F32)
            a, carry = key_block(z, softplus(z), carry, None)
            pv = jnp.dot(a, v_ref[0, pl.ds(s0, tk), :], preferred_element_type=F32)
            return j - 1, carry, acc + split_heads(pv), jnp.max(carry)

        _, _, acc, _ = lax.while_loop(cond, body, (first - 1, carry, acc, top))
        gate = g_ref[0, pl.ds(t0, tq), :].astype(F32)
        o_ref[0, pl.ds(t0, tq), :] = (acc * gate).astype(BF16)

    def run(lo, hi, count, diag_only):
        def step(p, _):
            i0 = lo + p * count
            units = [(i0 + u, part) for u in range(count) for part in range(len(SB_WIDE_PARTS))]
            for unit, st in zip(units, wide_steps(units, diag_only)):
                finish(unit, *st)
            return 0
        if hi > lo:
            lax.fori_loop(0, (hi - lo) // count, step, 0)

    nq = seq // tk
    widest = max(SB_WIDE_PARTS)
    n_lead = widest - 1 + (nq - (widest - 1)) % SB_UNROLL
    run(0, widest - 1, 1, False)
    run(widest - 1, n_lead, 1, True)
    run(n_lead, nq, SB_UNROLL, True)


def _sb_attention(q, k, v, g, tri):
    bsz, seq, width = q.shape
    spec = pl.BlockSpec((1, seq, LANES), lambda b, h: (b, 0, h))
    return pl.pallas_call(
        _sb_kernel,
        grid=(bsz, width // LANES),
        in_specs=[spec, spec, spec, spec, pl.BlockSpec(tri.shape, lambda b, h: (0, 0))],
        out_specs=spec,
        out_shape=jax.ShapeDtypeStruct((bsz, seq, width), BF16),
        compiler_params=pltpu.CompilerParams(dimension_semantics=("parallel", "parallel"),
                                             vmem_limit_bytes=VMEM_LIMIT),
        name="sb_attention",
    )(q, k, v, g, tri)


def _ml_kernel(q_ref, k_ref, v_ref, g_ref, gates_ref, gain_ref, sel_ref, o_ref,
               cn_ref, m_ref, x_ref):
    ts = q_ref.shape[1]
    nh = ML_HEADS
    dh = q_ref.shape[2] // nh
    clen = ML_CHUNK
    row = lax.broadcasted_iota(jnp.int32, (clen, clen), 0)
    col = lax.broadcasted_iota(jnp.int32, (clen, clen), 1)
    causal = col <= row
    ones_blk = jnp.ones((clen, dh), BF16)
    mean_mat = jnp.full((dh, dh), 1.0 / dh, BF16)

    @pl.when(pl.program_id(1) == 0)
    def _():
        cn_ref[...] = jnp.zeros(cn_ref.shape, F32)
        m_ref[...] = jnp.zeros(m_ref.shape, F32)

    heads = range(nh)
    lanes = [slice(h * dh, (h + 1) * dh) for h in heads]

    def chunk_phases(c, slot):
        t0 = pl.multiple_of(c * clen, clen)
        st = {}

        def gate_rows(group):
            return gates_ref[0, group * GATE_ROWS:group * GATE_ROWS + nh, pl.ds(t0, clen)]

        def col_tile(vec, h):
            j = (vec * nh + h) * LANES
            return st["bcast"][:, j:j + LANES]

        def scores_and_gates():
            st["q"] = [q_ref[0, pl.ds(t0, clen), lanes[h]] for h in heads]
            st["k"] = [k_ref[0, pl.ds(t0, clen), lanes[h]] for h in heads]
            st["qk"] = [lax.dot_general(st["q"][h], st["k"][h], NT_DIMS,
                                        preferred_element_type=F32) for h in heads]
            u, bcum, u_max, b_last, u_top = (gate_rows(g) for g in range(GATE_GROUPS))
            m_prev = m_ref[...]
            m_rel = jnp.maximum(m_prev, u_max)
            m_new = b_last + jnp.maximum(m_prev, u_top)
            m_ref[...] = m_new
            st["u"] = u
            st["g_c"] = jnp.exp(b_last + m_prev - m_new)
            x_ref[slot, 0 * nh:1 * nh] = -m_rel
            x_ref[slot, 1 * nh:2 * nh] = jnp.exp(m_prev - m_rel)
            x_ref[slot, 2 * nh:3 * nh] = jnp.exp(jnp.minimum(-(bcum + m_rel), ML_MAX_EXP_ARG))
            x_ref[slot, 3 * nh:4 * nh] = jnp.exp(b_last + u - m_new)
            st["bcast"] = lax.dot_general(_split3(x_ref[slot]), sel_ref[...], TN_DIMS,
                                          preferred_element_type=F32)

        def outputs():
            st["cn"] = [cn_ref[h] for h in heads]
            inter = [jnp.dot(st["q"][h], st["cn"][h].astype(BF16), preferred_element_type=F32)
                     for h in heads]
            st["v_aug"] = [jnp.concatenate([v_ref[0, pl.ds(t0, clen), lanes[h]], ones_blk], axis=1)
                           for h in heads]
            sc = [st["qk"][h] * jnp.exp(jnp.where(causal, st["u"][h:h + 1, :] + col_tile(0, h),
                                                  -jnp.inf)) for h in heads]
            intra = [jnp.dot(sc[h].astype(BF16), st["v_aug"][h], preferred_element_type=F32)
                     for h in heads]
            st["h_out"] = []
            for h in heads:
                g_inter = col_tile(1, h)
                num = intra[h][:, :dh] + g_inter * inter[h][:, :dh]
                den = intra[h][:, dh:] + g_inter * inter[h][:, dh:]
                st["h_out"].append(num / jnp.maximum(jnp.abs(den), col_tile(2, h)))
            st["ms"] = [jnp.dot((st["h_out"][h] * st["h_out"][h]).astype(BF16), mean_mat,
                                preferred_element_type=F32) for h in heads]

        def state_update():
            kw = [(st["k"][h].astype(F32) * col_tile(3, h)).astype(BF16) for h in heads]
            upd = [lax.dot_general(kw[h], st["v_aug"][h], TN_DIMS, preferred_element_type=F32)
                   for h in heads]
            for h in heads:
                decay = jnp.concatenate([st["g_c"][h:h + 1, :]] * (2 * dh // clen), axis=1)
                cn_ref[h] = decay * st["cn"][h] + upd[h]

        def store():
            for h in heads:
                hn = st["h_out"][h] * lax.rsqrt(st["ms"][h] + EPS) * gain_ref[:, lanes[h]]
                gate = g_ref[0, pl.ds(t0, clen), lanes[h]].astype(F32)
                o_ref[0, pl.ds(t0, clen), lanes[h]] = (hn * gate).astype(BF16)

        return scores_and_gates, outputs, state_update, store

    def chunk_pair(p, _):
        a = chunk_phases(2 * p, 0)
        b = chunk_phases(2 * p + 1, 1)
        for phase in (a[0], b[0], a[1], a[2], b[1], a[3], b[2], b[3]):
            phase()
        return 0

    lax.fori_loop(0, ts // (2 * clen), chunk_pair, 0)


def _mlstm(q, k, v, g, gates, gain):
    bsz, seq, width = q.shape
    dh = width // ML_HEADS
    ts = min(ML_TILE, seq)
    nvec = 4 * ML_HEADS
    sel = (jnp.arange(4 * nvec)[:, None] % nvec == jnp.arange(nvec * LANES)[None, :] // LANES)
    spec = pl.BlockSpec((1, ts, width), lambda b, s: (b, s, 0))
    return pl.pallas_call(
        _ml_kernel,
        grid=(bsz, seq // ts),
        in_specs=[spec, spec, spec, spec,
                  pl.BlockSpec((1, GATE_GROUPS * GATE_ROWS, ts), lambda b, s: (b, 0, s)),
                  pl.BlockSpec((1, width), lambda b, s: (0, 0)),
                  pl.BlockSpec((4 * nvec, nvec * LANES), lambda b, s: (0, 0))],
        out_specs=spec,
        out_shape=jax.ShapeDtypeStruct((bsz, seq, width), BF16),
        scratch_shapes=[pltpu.VMEM((ML_HEADS, dh, 2 * dh), F32),
                        pltpu.VMEM((ML_HEADS, ML_CHUNK), F32),
                        pltpu.VMEM((2, nvec, ML_CHUNK), F32)],
        compiler_params=pltpu.CompilerParams(dimension_semantics=("parallel", "arbitrary"),
                                             vmem_limit_bytes=VMEM_LIMIT),
        name="mlstm",
    )(q, k, v, g, gates, gain.reshape(1, width), sel.astype(BF16))


def _outproj_kernel(ysb_ref, yml_ref, x_ref, mod_ref, w_ref, o_ref, *, width):
    y = (jnp.dot(ysb_ref[0], w_ref[:width, :], preferred_element_type=F32)
         + jnp.dot(yml_ref[0], w_ref[width:, :], preferred_element_type=F32))
    o_ref[0] = x_ref[0] + mod_ref[2, 0] * y


def _outproj(ysb, yml, x, mod, w_out):
    bsz, seq, d = x.shape
    width = ysb.shape[2]
    ts = min(OUT_TILE, seq)
    ytile = pl.BlockSpec((1, ts, width), lambda b, s: (b, s, 0))
    xtile = pl.BlockSpec((1, ts, d), lambda b, s: (b, s, 0))
    return pl.pallas_call(
        functools.partial(_outproj_kernel, width=width),
        grid=(bsz, seq // ts),
        in_specs=[ytile, ytile, xtile,
                  pl.BlockSpec((3, 1, 1, d), lambda b, s: (0, b, 0, 0)),
                  pl.BlockSpec(w_out.shape, lambda b, s: (0, 0))],
        out_specs=xtile,
        out_shape=jax.ShapeDtypeStruct(x.shape, x.dtype),
        compiler_params=pltpu.CompilerParams(dimension_semantics=("parallel", "parallel"),
                                             vmem_limit_bytes=VMEM_LIMIT),
        name="outproj",
    )(ysb, yml, x, mod.reshape(3, bsz, 1, d), w_out)


def _layer(x, c, w_ada, b_ada, norm_gain, w_in, b_gates, q_norm_gain, k_norm_gain,
           conv_w, conv_b, ml_norm_gain, w_out):
    bsz, seq, d = x.shape
    width = d // 2
    assert d % (2 * LANES) == 0 and width == ML_HEADS * LANES
    assert seq % IN_TILE == 0 and seq % SB_TILE == 0 and seq % ML_TILE == 0
    assert ML_TILE % (2 * ML_CHUNK) == 0 and ML_CHUNK == LANES
    assert seq >= max(SB_WIDE_PARTS) * SB_TILE and SB_TILE % len(SB_WIDE_PARTS) == 0
    nmain = 9 * width
    w_main = w_in.astype(BF16)
    pad = ((0, GATE_ROWS - ML_HEADS), (0, 0))
    wg = w_in[:, nmain:].T
    wg_t = jnp.concatenate([jnp.pad(wg[:ML_HEADS], pad), jnp.pad(wg[ML_HEADS:], pad)]).astype(BF16)
    bg = b_gates.reshape(2 * ML_HEADS, 1)
    bg = jnp.concatenate([jnp.pad(bg[:ML_HEADS], pad), jnp.pad(bg[ML_HEADS:], pad)])
    heads = width // SB_HEAD_DIM
    head_id = jnp.arange(min(MXU_DIM, width)) // SB_HEAD_DIM
    pmat = jnp.where(head_id[:, None] == head_id[None, :], 1.0 / SB_HEAD_DIM, 0.0).astype(BF16)
    qg = jnp.tile(q_norm_gain, heads).reshape(1, width)
    kg = jnp.tile(k_norm_gain, heads).reshape(1, width)
    kk = jnp.arange(SB_TILE)
    tri = -jnp.concatenate([(kk[:, None] > kk[None, :]).astype(BF16),
                            jnp.ones((SB_TILE, SB_TILE), BF16)], axis=1)

    mod = _mod(c, w_ada, b_ada)
    (qsb, ksb, vsb, gsb, qml, kml, vml, gml, gates) = _inproj(
        x, mod, norm_gain, w_main, wg_t, bg, pmat, qg, kg, conv_w, conv_b)
    ysb = _sb_attention(qsb, ksb, vsb, gsb, tri)
    yml = _mlstm(qml, kml, vml, gml, _gate_scan(gates), ml_norm_gain)
    return _outproj(ysb, yml, x, mod, w_out.astype(BF16))


def kernel(x, c, w_ada, b_ada, norm_gain, w_in, b_gates, q_norm_gain, k_norm_gain, conv_w,
           conv_b, ml_norm_gain, w_out):
    h = x
    for layer in range(w_in.shape[0]):
        h = _layer(h, c, w_ada[layer], b_ada[layer], norm_gain[layer], w_in[layer],
                   b_gates[layer], q_norm_gain[layer], k_norm_gain[layer], conv_w[layer],
                   conv_b[layer], ml_norm_gain[layer], w_out[layer])
    return h
```

```python
import functools
import math

import jax
import jax.numpy as jnp
from jax import lax
from jax.experimental import pallas as pl
from jax.experimental.pallas import tpu as pltpu

F32 = jnp.float32
BF16 = jnp.bfloat16

SB_HEAD_DIM = 64
ML_HEADS = 4
CONV_WIDTH = 4
EPS = 1e-6

LANES = 128
SUBLANES = 8
GATE_ROWS = SUBLANES
GATE_GROUPS = 5
MXU_DIM = 256
GATE_SCAN_SPAN = MXU_DIM
GATE_SCAN_BATCH = 4
VMEM_LIMIT = 56 * 1024 * 1024

IN_TILE = 1024
OUT_TILE = 2048
SB_TILE = 128
SB_WIDE_PARTS = (3,)
SB_UNROLL = 6
ML_CHUNK = 128
ML_TILE = 2048
ML_MAX_EXP_ARG = 88.0
SB_LOG2_STICK_FLOOR = -150.0

NT_DIMS = (((1,), (1,)), ((), ()))
TN_DIMS = (((0,), (0,)), ((), ()))


def _sigmoid(u):
    return 0.5 + 0.5 * jnp.tanh(0.5 * u)


def _silu(u):
    h = 0.5 * u
    return h + h * jnp.tanh(h)


def _log_sigmoid(u):
    return jnp.minimum(u, 0.0) - jnp.log1p(jnp.exp(-jnp.abs(u)))


def _mod_kernel(c_ref, w_ref, b_ref, o_ref):
    c = c_ref[...]
    o_ref[0] = jnp.dot(_silu(c), w_ref[...], preferred_element_type=F32) + b_ref[0]


def _mod(c, w_ada, b_ada):
    bsz, d = c.shape
    return pl.pallas_call(
        _mod_kernel,
        grid=(3,),
        in_specs=[pl.BlockSpec((bsz, d), lambda j: (0, 0)),
                  pl.BlockSpec((d, d), lambda j: (0, j)),
                  pl.BlockSpec((1, 1, d), lambda j: (j, 0, 0))],
        out_specs=pl.BlockSpec((1, bsz, d), lambda j: (j, 0, 0)),
        out_shape=jax.ShapeDtypeStruct((3, bsz, d), F32),
        compiler_params=pltpu.CompilerParams(dimension_semantics=("arbitrary",),
                                             vmem_limit_bytes=VMEM_LIMIT),
        name="adaln_mod",
    )(c, w_ada, b_ada.reshape(3, 1, d))


def _split3(x):
    p0 = x.astype(BF16).astype(F32)
    r1 = x - p0
    p1 = r1.astype(BF16).astype(F32)
    return jnp.concatenate([p0, p1, r1 - p1, jnp.zeros_like(x)], axis=0).astype(BF16)


def _exact_rows_dot(x, mat):
    r = x.shape[0]
    y = jnp.dot(_split3(x), mat, preferred_element_type=F32)
    return y[0:r] + y[r:2 * r] + y[2 * r:3 * r]


def _inproj_kernel(x_ref, mod_ref, gain_ref, w_ref, wg_ref, bg_ref, p_ref, qg_ref, kg_ref,
                   cw_ref, cb_ref,
                   qsb_ref, ksb_ref, vsb_ref, gsb_ref, qml_ref, kml_ref, vml_ref, gml_ref,
                   gates_ref, tail_ref, *, width, ml_scale):
    ts = x_ref.shape[1]

    @pl.when(pl.program_id(1) == 0)
    def _():
        tail_ref[...] = jnp.zeros(tail_ref.shape, F32)

    x = x_ref[0]
    ms = jnp.mean(x * x, axis=-1, keepdims=True)
    shift = mod_ref[0, 0]
    scale = mod_ref[1, 0]
    hn = ((x * lax.rsqrt(ms + EPS)) * (gain_ref[...] * (1.0 + scale)) + shift).astype(BF16)

    gates_ref[0] = (lax.dot_general(wg_ref[...], hn, NT_DIMS, preferred_element_type=F32)
                    + bg_ref[...])

    def proj(g):
        return jnp.dot(hn, w_ref[:, g * width:(g + 1) * width], preferred_element_type=F32)


    head_rows = lax.broadcasted_iota(jnp.int32, (SUBLANES, width), 0)
    for idx, (g, out_ref, sc) in enumerate(((4, qml_ref, ml_scale), (5, kml_ref, 1.0))):
        gcols = slice(idx * width, (idx + 1) * width)
        u = proj(g)
        tail = tail_ref[idx]
        acc = cb_ref[:, gcols] + cw_ref[CONV_WIDTH - 1:CONV_WIDTH, gcols] * u
        for k in range(1, CONV_WIDTH):
            delayed = pltpu.roll(u, k, 0)
            head = jnp.where(head_rows < k, pltpu.roll(tail, k, 0), delayed[:SUBLANES])
            delayed = jnp.concatenate([head, delayed[SUBLANES:]], axis=0)
            acc = acc + cw_ref[CONV_WIDTH - 1 - k:CONV_WIDTH - k, gcols] * delayed
        out_ref[0] = (_silu(acc) * sc).astype(BF16)
        tail_ref[idx] = u[ts - SUBLANES:]

    gml_ref[0] = (_sigmoid(proj(7)) * _silu(proj(8))).astype(BF16)
    gsb_ref[0] = _silu(proj(3)).astype(BF16)

    def head_mean_sq(v):
        sq = (v * v).astype(BF16)
        span = p_ref.shape[0]
        return jnp.concatenate([jnp.dot(sq[:, j:j + span], p_ref[...], preferred_element_type=F32)
                                for j in range(0, width, span)], axis=1)

    for g, gn_ref, out_ref, sc in ((0, qg_ref, qsb_ref, math.log2(math.e) / math.sqrt(SB_HEAD_DIM)),
                                   (1, kg_ref, ksb_ref, 1.0)):
        u = proj(g)
        out_ref[0] = (u * lax.rsqrt(head_mean_sq(u) + EPS) * (gn_ref[...] * sc)).astype(BF16)
    vsb_ref[0] = proj(2).astype(BF16)
    vml_ref[0] = proj(6).astype(BF16)


def _inproj(x, mod, norm_gain, w_main, wg_t, b_gates, pmat, qg, kg, conv_w, conv_b):
    bsz, seq, d = x.shape
    width = d // 2
    ts = min(IN_TILE, seq)
    ngroups = 9
    const2 = lambda b, s: (0, 0)
    tile_spec = pl.BlockSpec((1, ts, width), lambda b, s: (b, s, 0))
    act = jax.ShapeDtypeStruct((bsz, seq, width), BF16)
    kern = functools.partial(_inproj_kernel, width=width,
                             ml_scale=1.0 / math.sqrt(width // ML_HEADS))
    return pl.pallas_call(
        kern,
        grid=(bsz, seq // ts),
        in_specs=[pl.BlockSpec((1, ts, d), lambda b, s: (b, s, 0)),
                  pl.BlockSpec((3, 1, 1, d), lambda b, s: (0, b, 0, 0)),
                  pl.BlockSpec((1, d), const2),
                  pl.BlockSpec((d, ngroups * width), const2, pipeline_mode=pl.Buffered(1)),
                  pl.BlockSpec((2 * GATE_ROWS, d), const2),
                  pl.BlockSpec((2 * GATE_ROWS, 1), const2),
                  pl.BlockSpec(pmat.shape, const2),
                  pl.BlockSpec((1, width), const2),
                  pl.BlockSpec((1, width), const2),
                  pl.BlockSpec((CONV_WIDTH, 2 * width), const2),
                  pl.BlockSpec((1, 2 * width), const2)],
        out_specs=[tile_spec] * 8 + [pl.BlockSpec((1, 2 * GATE_ROWS, ts), lambda b, s: (b, 0, s))],
        out_shape=[act] * 8 + [jax.ShapeDtypeStruct((bsz, 2 * GATE_ROWS, seq), F32)],
        scratch_shapes=[pltpu.VMEM((2, SUBLANES, width), F32)],
        compiler_params=pltpu.CompilerParams(dimension_semantics=("parallel", "arbitrary"),
                                             vmem_limit_bytes=VMEM_LIMIT),
        name="inproj",
    )(x, mod.reshape(3, bsz, 1, d), norm_gain.reshape(1, d), w_main, wg_t,
      b_gates, pmat, qg, kg, conv_w, conv_b.reshape(1, 2 * width))


def _gate_scan_kernel(gt_ref, scan_ref, o_ref):
    nb, _, seq = gt_ref.shape
    span = scan_ref.shape[1]
    rows = nb * GATE_ROWS

    def per_span(x, mat):
        return jnp.concatenate([_exact_rows_dot(x[:, j:j + span], mat)
                                for j in range(0, seq, span)], axis=1)

    i_pre = gt_ref[:, :GATE_ROWS, :].reshape(rows, seq)
    f_pre = gt_ref[:, GATE_ROWS:, :].reshape(rows, seq)
    bcum = per_span(_log_sigmoid(f_pre), scan_ref[0])
    u = i_pre - bcum
    lane_in_chunk = lax.broadcasted_iota(jnp.int32, (rows, seq), 1) % ML_CHUNK
    u_max = u
    step = 1
    while step < ML_CHUNK:
        shifted = jnp.where(lane_in_chunk >= step, pltpu.roll(u_max, step, 1), -jnp.inf)
        u_max = jnp.maximum(u_max, shifted)
        step *= 2
    ends = per_span(jnp.concatenate([bcum, u_max], axis=0), scan_ref[1])
    groups = (u, bcum, u_max, ends[:rows], ends[rows:])
    for g, val in enumerate(groups):
        o_ref[:, g * GATE_ROWS:(g + 1) * GATE_ROWS, :] = val.reshape(nb, GATE_ROWS, seq)


def _gate_scan(gt):
    bsz, rows, seq = gt.shape
    span = min(GATE_SCAN_SPAN, seq)
    t = jnp.arange(span)
    same_chunk = t[:, None] // ML_CHUNK == t[None, :] // ML_CHUNK
    scan_mats = jnp.stack([same_chunk & (t[:, None] <= t[None, :]),
                           same_chunk & (t[:, None] % ML_CHUNK == ML_CHUNK - 1)]).astype(BF16)
    nb = math.gcd(bsz, GATE_SCAN_BATCH)
    return pl.pallas_call(
        _gate_scan_kernel,
        grid=(bsz // nb,),
        in_specs=[pl.BlockSpec((nb, rows, seq), lambda b: (b, 0, 0)),
                  pl.BlockSpec((2, span, span), lambda b: (0, 0, 0))],
        out_specs=pl.BlockSpec((nb, GATE_GROUPS * GATE_ROWS, seq), lambda b: (b, 0, 0)),
        out_shape=jax.ShapeDtypeStruct((bsz, GATE_GROUPS * GATE_ROWS, seq), F32),
        compiler_params=pltpu.CompilerParams(dimension_semantics=("parallel",),
                                             vmem_limit_bytes=VMEM_LIMIT),
        name="gate_scan",
    )(gt, scan_mats)


def _sb_kernel(q_ref, k_ref, v_ref, g_ref, tri_ref, o_ref):
    seq = q_ref.shape[1]
    tk = SB_TILE
    tq = tk // len(SB_WIDE_PARTS)
    lane_a = lax.broadcasted_iota(jnp.int32, (tq, LANES), 1) < SB_HEAD_DIM
    rows = lax.broadcasted_iota(jnp.int32, (2 * tq, tk), 0)
    cols = lax.broadcasted_iota(jnp.int32, (2 * tq, tk), 1)
    rel = cols - jnp.where(rows < tq, rows, rows - tq)
    ntri = tri_ref[...]

    def softplus(z):
        return jnp.maximum(z, 0.0) + jnp.log2(1.0 + jnp.exp2(-jnp.abs(z)))

    def key_block(z, sp, carry, mask):
        sp_in = sp if mask is None else jnp.where(mask, sp, 0.0)
        r = jnp.dot(sp_in.astype(BF16), ntri, preferred_element_type=F32)
        a = jnp.exp2(z - sp + carry + r[:, :tk])
        if mask is not None:
            a = jnp.where(mask, a, 0.0)
        return a.astype(BF16), carry + r[:, tk:]

    def split_heads(pv):
        return jnp.where(lane_a, pv[:tq], pv[tq:])

    def wide_phases(units, diag_only):
        idx = range(len(units))
        wide = [SB_WIDE_PARTS[part] for _, part in units]
        st = {}

        def scores():
            st["qab"], st["first"], st["start"], st["z"] = [], [], [], []
            for u, (i, part) in enumerate(units):
                t0 = pl.multiple_of(i * tk + part * tq, tq)
                q2 = q_ref[0, pl.ds(t0, tq), :]
                zero = jnp.zeros_like(q2)
                qab = jnp.concatenate([jnp.where(lane_a, q2, zero), jnp.where(lane_a, zero, q2)],
                                      axis=0)
                first = i - (wide[u] - 1) if diag_only else jnp.maximum(i - (wide[u] - 1), 0)
                start = pl.multiple_of(first * tk, tk)
                st["qab"].append(qab)
                st["first"].append(first)
                st["start"].append(start)
                st["z"].append(lax.dot_general(qab, k_ref[0, pl.ds(start, wide[u] * tk), :],
                                               NT_DIMS, preferred_element_type=F32))

        def log_terms():
            st["sp"] = [softplus(z) for z in st["z"]]

        def weights():
            carry = [jnp.zeros((2 * tq, tk), F32) for _ in idx]
            a_parts = [[None] * wide[u] for u in idx]
            for back in range(max(wide)):
                for u, (i, part) in enumerate(units):
                    m = wide[u] - 1 - back
                    if m < 0:
                        continue
                    if diag_only:
                        mask = rel < part * tq if back == 0 else None
                    else:
                        mask = rel < (i * tk + part * tq - st["start"][u] - m * tk)
                    sl = slice(m * tk, (m + 1) * tk)
                    a_parts[u][m], carry[u] = key_block(st["z"][u][:, sl], st["sp"][u][:, sl],
                                                        carry[u], mask)
            st["a"] = [jnp.concatenate(parts, axis=1) for parts in a_parts]
            st["carry"] = carry

        def values():
            st["out"] = [(st["qab"][u], st["first"][u], st["carry"][u],
                          split_heads(jnp.dot(st["a"][u],
                                              v_ref[0, pl.ds(st["start"][u], wide[u] * tk), :],
                                              preferred_element_type=F32)),
                          jnp.max(st["carry"][u])) for u in idx]

        return [scores, log_terms, weights, values], st

    def wide_steps(units, diag_only):
        if len(units) <= len(SB_WIDE_PARTS):
            phases, st = wide_phases(units, diag_only)
            for phase in phases:
                phase()
            return st["out"]
        (pa, sta), (pb, stb) = (wide_phases(units[:len(units) // 2], diag_only),
                                wide_phases(units[len(units) // 2:], diag_only))
        for phase in (pa[0], pa[1], pb[0], pa[2], pb[1], pa[3], pb[2], pb[3]):
            phase()
        return sta["out"] + stb["out"]

    def finish(unit, qab, first, carry, acc, top):
        t0 = pl.multiple_of(unit[0] * tk + unit[1] * tq, tq)

        def cond(st):
            j, _, _, top = st
            return jnp.logical_and(j >= 0, top > SB_LOG2_STICK_FLOOR)

        def body(st):
            j, carry, acc, _ = st
            s0 = pl.multiple_of(j * tk, tk)
            z = lax.dot_general(qab, k_ref[0, pl.ds(s0, tk), :], NT_DIMS,
                                preferred_element_type=F32)
            a, carry = key_block(z, softplus(z), carry, None)
            pv = jnp.dot(a, v_ref[0, pl.ds(s0, tk), :], preferred_element_type=F32)
            return j - 1, carry, acc + split_heads(pv), jnp.max(carry)

        _, _, acc, _ = lax.while_loop(cond, body, (first - 1, carry, acc, top))
        gate = g_ref[0, pl.ds(t0, tq), :].astype(F32)
        o_ref[0, pl.ds(t0, tq), :] = (acc * gate).astype(BF16)

    def run(lo, hi, count, diag_only):
        def step(p, _):
            i0 = lo + p * count
            units = [(i0 + u, part) for u in range(count) for part in range(len(SB_WIDE_PARTS))]
            for unit, st in zip(units, wide_steps(units, diag_only)):
                finish(unit, *st)
            return 0
        if hi > lo:
            lax.fori_loop(0, (hi - lo) // count, step, 0)

    nq = seq // tk
    widest = max(SB_WIDE_PARTS)
    n_lead = widest - 1 + (nq - (widest - 1)) % SB_UNROLL
    run(0, widest - 1, 1, False)
    run(widest - 1, n_lead, 1, True)
    run(n_lead, nq, SB_UNROLL, True)


def _sb_attention(q, k, v, g, tri):
    bsz, seq, width = q.shape
    spec = pl.BlockSpec((1, seq, LANES), lambda b, h: (b, 0, h))
    return pl.pallas_call(
        _sb_kernel,
        grid=(bsz, width // LANES),
        in_specs=[spec, spec, spec, spec, pl.BlockSpec(tri.shape, lambda b, h: (0, 0))],
        out_specs=spec,
        out_shape=jax.ShapeDtypeStruct((bsz, seq, width), BF16),
        compiler_params=pltpu.CompilerParams(dimension_semantics=("parallel", "parallel"),
                                             vmem_limit_bytes=VMEM_LIMIT),
        name="sb_attention",
    )(q, k, v, g, tri)


def _ml_kernel(q_ref, k_ref, v_ref, g_ref, gates_ref, gain_ref, sel_ref, o_ref,
               cn_ref, m_ref, x_ref):
    ts = q_ref.shape[1]
    nh = ML_HEADS
    dh = q_ref.shape[2] // nh
    clen = ML_CHUNK
    row = lax.broadcasted_iota(jnp.int32, (clen, clen), 0)
    col = lax.broadcasted_iota(jnp.int32, (clen, clen), 1)
    causal = col <= row
    ones_blk = jnp.ones((clen, dh), BF16)
    mean_mat = jnp.full((dh, dh), 1.0 / dh, BF16)

    @pl.when(pl.program_id(1) == 0)
    def _():
        cn_ref[...] = jnp.zeros(cn_ref.shape, F32)
        m_ref[...] = jnp.zeros(m_ref.shape, F32)

    heads = range(nh)
    lanes = [slice(h * dh, (h + 1) * dh) for h in heads]

    def chunk_phases(c, slot):
        t0 = pl.multiple_of(c * clen, clen)
        st = {}

        def gate_rows(group):
            return gates_ref[0, group * GATE_ROWS:group * GATE_ROWS + nh, pl.ds(t0, clen)]

        def col_tile(vec, h):
            j = (vec * nh + h) * LANES
            return st["bcast"][:, j:j + LANES]

        def scores_and_gates():
            st["q"] = [q_ref[0, pl.ds(t0, clen), lanes[h]] for h in heads]
            st["k"] = [k_ref[0, pl.ds(t0, clen), lanes[h]] for h in heads]
            st["qk"] = [lax.dot_general(st["q"][h], st["k"][h], NT_DIMS,
                                        preferred_element_type=F32) for h in heads]
            u, bcum, u_max, b_last, u_top = (gate_rows(g) for g in range(GATE_GROUPS))
            m_prev = m_ref[...]
            m_rel = jnp.maximum(m_prev, u_max)
            m_new = b_last + jnp.maximum(m_prev, u_top)
            m_ref[...] = m_new
            st["u"] = u
            st["g_c"] = jnp.exp(b_last + m_prev - m_new)
            x_ref[slot, 0 * nh:1 * nh] = -m_rel
            x_ref[slot, 1 * nh:2 * nh] = jnp.exp(m_prev - m_rel)
            x_ref[slot, 2 * nh:3 * nh] = jnp.exp(jnp.minimum(-(bcum + m_rel), ML_MAX_EXP_ARG))
            x_ref[slot, 3 * nh:4 * nh] = jnp.exp(b_last + u - m_new)
            st["bcast"] = lax.dot_general(_split3(x_ref[slot]), sel_ref[...], TN_DIMS,
                                          preferred_element_type=F32)

        def outputs():
            st["cn"] = [cn_ref[h] for h in heads]
            inter = [jnp.dot(st["q"][h], st["cn"][h].astype(BF16), preferred_element_type=F32)
                     for h in heads]
            st["v_aug"] = [jnp.concatenate([v_ref[0, pl.ds(t0, clen), lanes[h]], ones_blk], axis=1)
                           for h in heads]
            sc = [st["qk"][h] * jnp.exp(jnp.where(causal, st["u"][h:h + 1, :] + col_tile(0, h),
                                                  -jnp.inf)) for h in heads]
            intra = [jnp.dot(sc[h].astype(BF16), st["v_aug"][h], preferred_element_type=F32)
                     for h in heads]
            st["h_out"] = []
            for h in heads:
                g_inter = col_tile(1, h)
                num = intra[h][:, :dh] + g_inter * inter[h][:, :dh]
                den = intra[h][:, dh:] + g_inter * inter[h][:, dh:]
                st["h_out"].append(num / jnp.maximum(jnp.abs(den), col_tile(2, h)))
            st["ms"] = [jnp.dot((st["h_out"][h] * st["h_out"][h]).astype(BF16), mean_mat,
                                preferred_element_type=F32) for h in heads]

        def state_update():
            kw = [(st["k"][h].astype(F32) * col_tile(3, h)).astype(BF16) for h in heads]
            upd = [lax.dot_general(kw[h], st["v_aug"][h], TN_DIMS, preferred_element_type=F32)
                   for h in heads]
            for h in heads:
                decay = jnp.concatenate([st["g_c"][h:h + 1, :]] * (2 * dh // clen), axis=1)
                cn_ref[h] = decay * st["cn"][h] + upd[h]

        def store():
            for h in heads:
                hn = st["h_out"][h] * lax.rsqrt(st["ms"][h] + EPS) * gain_ref[:, lanes[h]]
                gate = g_ref[0, pl.ds(t0, clen), lanes[h]].astype(F32)
                o_ref[0, pl.ds(t0, clen), lanes[h]] = (hn * gate).astype(BF16)

        return scores_and_gates, outputs, state_update, store

    def chunk_pair(p, _):
        a = chunk_phases(2 * p, 0)
        b = chunk_phases(2 * p + 1, 1)
        for phase in (a[0], b[0], a[1], a[2], b[1], a[3], b[2], b[3]):
            phase()
        return 0

    lax.fori_loop(0, ts // (2 * clen), chunk_pair, 0)


def _mlstm(q, k, v, g, gates, gain):
    bsz, seq, width = q.shape
    dh = width // ML_HEADS
    ts = min(ML_TILE, seq)
    nvec = 4 * ML_HEADS
    sel = (jnp.arange(4 * nvec)[:, None] % nvec == jnp.arange(nvec * LANES)[None, :] // LANES)
    spec = pl.BlockSpec((1, ts, width), lambda b, s: (b, s, 0))
    return pl.pallas_call(
        _ml_kernel,
        grid=(bsz, seq // ts),
        in_specs=[spec, spec, spec, spec,
                  pl.BlockSpec((1, GATE_GROUPS * GATE_ROWS, ts), lambda b, s: (b, 0, s)),
                  pl.BlockSpec((1, width), lambda b, s: (0, 0)),
                  pl.BlockSpec((4 * nvec, nvec * LANES), lambda b, s: (0, 0))],
        out_specs=spec,
        out_shape=jax.ShapeDtypeStruct((bsz, seq, width), BF16),
        scratch_shapes=[pltpu.VMEM((ML_HEADS, dh, 2 * dh), F32),
                        pltpu.VMEM((ML_HEADS, ML_CHUNK), F32),
                        pltpu.VMEM((2, nvec, ML_CHUNK), F32)],
        compiler_params=pltpu.CompilerParams(dimension_semantics=("parallel", "arbitrary"),
                                             vmem_limit_bytes=VMEM_LIMIT),
        name="mlstm",
    )(q, k, v, g, gates, gain.reshape(1, width), sel.astype(BF16))


def _outproj_kernel(ysb_ref, yml_ref, x_ref, mod_ref, w_ref, o_ref, *, width):
    y = (jnp.dot(ysb_ref[0], w_ref[:width, :], preferred_element_type=F32)
         + jnp.dot(yml_ref[0], w_ref[width:, :], preferred_element_type=F32))
    o_ref[0] = x_ref[0] + mod_ref[2, 0] * y


def _outproj(ysb, yml, x, mod, w_out):
    bsz, seq, d = x.shape
    width = ysb.shape[2]
    ts = min(OUT_TILE, seq)
    ytile = pl.BlockSpec((1, ts, width), lambda b, s: (b, s, 0))
    xtile = pl.BlockSpec((1, ts, d), lambda b, s: (b, s, 0))
    return pl.pallas_call(
        functools.partial(_outproj_kernel, width=width),
        grid=(bsz, seq // ts),
        in_specs=[ytile, ytile, xtile,
                  pl.BlockSpec((3, 1, 1, d), lambda b, s: (0, b, 0, 0)),
                  pl.BlockSpec(w_out.shape, lambda b, s: (0, 0))],
        out_specs=xtile,
        out_shape=jax.ShapeDtypeStruct(x.shape, x.dtype),
        compiler_params=pltpu.CompilerParams(dimension_semantics=("parallel", "parallel"),
                                             vmem_limit_bytes=VMEM_LIMIT),
        name="outproj",
    )(ysb, yml, x, mod.reshape(3, bsz, 1, d), w_out)


def _layer(x, c, w_ada, b_ada, norm_gain, w_in, b_gates, q_norm_gain, k_norm_gain,
           conv_w, conv_b, ml_norm_gain, w_out):
    bsz, seq, d = x.shape
    width = d // 2
    assert d % (2 * LANES) == 0 and width == ML_HEADS * LANES
    assert seq % IN_TILE == 0 and seq % SB_TILE == 0 and seq % ML_TILE == 0
    assert ML_TILE % (2 * ML_CHUNK) == 0 and ML_CHUNK == LANES
    assert seq >= max(SB_WIDE_PARTS) * SB_TILE and SB_TILE % len(SB_WIDE_PARTS) == 0
    nmain = 9 * width
    w_main = w_in.astype(BF16)
    pad = ((0, GATE_ROWS - ML_HEADS), (0, 0))
    wg = w_in[:, nmain:].T
    wg_t = jnp.concatenate([jnp.pad(wg[:ML_HEADS], pad), jnp.pad(wg[ML_HEADS:], pad)]).astype(BF16)
    bg = b_gates.reshape(2 * ML_HEADS, 1)
    bg = jnp.concatenate([jnp.pad(bg[:ML_HEADS], pad), jnp.pad(bg[ML_HEADS:], pad)])
    heads = width // SB_HEAD_DIM
    head_id = jnp.arange(min(MXU_DIM, width)) // SB_HEAD_DIM
    pmat = jnp.where(head_id[:, None] == head_id[None, :], 1.0 / SB_HEAD_DIM, 0.0).astype(BF16)
    qg = jnp.tile(q_norm_gain, heads).reshape(1, width)
    kg = jnp.tile(k_norm_gain, heads).reshape(1, width)
    kk = jnp.arange(SB_TILE)
    tri = -jnp.concatenate([(kk[:, None] > kk[None, :]).astype(BF16),
                            jnp.ones((SB_TILE, SB_TILE), BF16)], axis=1)

    mod = _mod(c, w_ada, b_ada)
    (qsb, ksb, vsb, gsb, qml, kml, vml, gml, gates) = _inproj(
        x, mod, norm_gain, w_main, wg_t, bg, pmat, qg, kg, conv_w, conv_b)
    ysb = _sb_attention(qsb, ksb, vsb, gsb, tri)
    yml = _mlstm(qml, kml, vml, gml, _gate_scan(gates), ml_norm_gain)
    return _outproj(ysb, yml, x, mod, w_out.astype(BF16))


def kernel(x, c, w_ada, b_ada, norm_gain, w_in, b_gates, q_norm_gain, k_norm_gain, conv_w,
           conv_b, ml_norm_gain, w_out):
    h = x
    for layer in range(w_in.shape[0]):
        h = _layer(h, c, w_ada[layer], b_ada[layer], norm_gain[layer], w_in[layer],
                   b_gates[layer], q_norm_gain[layer], k_norm_gain[layer], conv_w[layer],
                   conv_b[layer], ml_norm_gain[layer], w_out[layer])
    return h
```

```python
import functools
import math

import jax
import jax.numpy as jnp
from jax import lax
from jax.experimental import pallas as pl
from jax.experimental.pallas import tpu as pltpu

F32 = jnp.float32
BF16 = jnp.bfloat16

SB_HEAD_DIM = 64
ML_HEADS = 4
CONV_WIDTH = 4
EPS = 1e-6

LANES = 128
SUBLANES = 8
GATE_ROWS = SUBLANES
GATE_GROUPS = 5
MXU_DIM = 256
GATE_SCAN_SPAN = MXU_DIM
GATE_SCAN_BATCH = 4
VMEM_LIMIT = 56 * 1024 * 1024

IN_TILE = 1024
OUT_TILE = 2048
SB_TILE = 128
SB_WIDE_PARTS = (3,)
SB_UNROLL = 6
ML_CHUNK = 128
ML_TILE = 2048
ML_MAX_EXP_ARG = 88.0
SB_LOG2_STICK_FLOOR = -150.0

NT_DIMS = (((1,), (1,)), ((), ()))
TN_DIMS = (((0,), (0,)), ((), ()))


def _sigmoid(u):
    return 0.5 + 0.5 * jnp.tanh(0.5 * u)


def _silu(u):
    h = 0.5 * u
    return h + h * jnp.tanh(h)


def _log_sigmoid(u):
    return jnp.minimum(u, 0.0) - jnp.log1p(jnp.exp(-jnp.abs(u)))


def _mod_kernel(c_ref, w_ref, b_ref, o_ref):
    c = c_ref[...]
    o_ref[0] = jnp.dot(_silu(c), w_ref[...], preferred_element_type=F32) + b_ref[0]


def _mod(c, w_ada, b_ada):
    bsz, d = c.shape
    return pl.pallas_call(
        _mod_kernel,
        grid=(3,),
        in_specs=[pl.BlockSpec((bsz, d), lambda j: (0, 0)),
                  pl.BlockSpec((d, d), lambda j: (0, j)),
                  pl.BlockSpec((1, 1, d), lambda j: (j, 0, 0))],
        out_specs=pl.BlockSpec((1, bsz, d), lambda j: (j, 0, 0)),
        out_shape=jax.ShapeDtypeStruct((3, bsz, d), F32),
        compiler_params=pltpu.CompilerParams(dimension_semantics=("arbitrary",),
                                             vmem_limit_bytes=VMEM_LIMIT),
        name="adaln_mod",
    )(c, w_ada, b_ada.reshape(3, 1, d))


def _split3(x):
    p0 = x.astype(BF16).astype(F32)
    r1 = x - p0
    p1 = r1.astype(BF16).astype(F32)
    return jnp.concatenate([p0, p1, r1 - p1, jnp.zeros_like(x)], axis=0).astype(BF16)


def _exact_rows_dot(x, mat):
    r = x.shape[0]
    y = jnp.dot(_split3(x), mat, preferred_element_type=F32)
    return y[0:r] + y[r:2 * r] + y[2 * r:3 * r]


def _inproj_kernel(x_ref, mod_ref, gain_ref, w_ref, wg_ref, bg_ref, p_ref, qg_ref, kg_ref,
                   cw_ref, cb_ref,
                   qsb_ref, ksb_ref, vsb_ref, gsb_ref, qml_ref, kml_ref, vml_ref, gml_ref,
                   gates_ref, tail_ref, *, width, ml_scale):
    ts = x_ref.shape[1]

    @pl.when(pl.program_id(1) == 0)
    def _():
        tail_ref[...] = jnp.zeros(tail_ref.shape, F32)

    x = x_ref[0]
    ms = jnp.mean(x * x, axis=-1, keepdims=True)
    shift = mod_ref[0, 0]
    scale = mod_ref[1, 0]
    hn = ((x * lax.rsqrt(ms + EPS)) * (gain_ref[...] * (1.0 + scale)) + shift).astype(BF16)

    gates_ref[0] = (lax.dot_general(wg_ref[...], hn, NT_DIMS, preferred_element_type=F32)
                    + bg_ref[...])

    def proj(g):
        return jnp.dot(hn, w_ref[:, g * width:(g + 1) * width], preferred_element_type=F32)


    head_rows = lax.broadcasted_iota(jnp.int32, (SUBLANES, width), 0)
    for idx, (g, out_ref, sc) in enumerate(((4, qml_ref, ml_scale), (5, kml_ref, 1.0))):
        gcols = slice(idx * width, (idx + 1) * width)
        u = proj(g)
        tail = tail_ref[idx]
        acc = cb_ref[:, gcols] + cw_ref[CONV_WIDTH - 1:CONV_WIDTH, gcols] * u
        for k in range(1, CONV_WIDTH):
            delayed = pltpu.roll(u, k, 0)
            head = jnp.where(head_rows < k, pltpu.roll(tail, k, 0), delayed[:SUBLANES])
            delayed = jnp.concatenate([head, delayed[SUBLANES:]], axis=0)
            acc = acc + cw_ref[CONV_WIDTH - 1 - k:CONV_WIDTH - k, gcols] * delayed
        out_ref[0] = (_silu(acc) * sc).astype(BF16)
        tail_ref[idx] = u[ts - SUBLANES:]

    gml_ref[0] = (_sigmoid(proj(7)) * _silu(proj(8))).astype(BF16)
    gsb_ref[0] = _silu(proj(3)).astype(BF16)

    def head_mean_sq(v):
        sq = (v * v).astype(BF16)
        span = p_ref.shape[0]
        return jnp.concatenate([jnp.dot(sq[:, j:j + span], p_ref[...], preferred_element_type=F32)
                                for j in range(0, width, span)], axis=1)

    for g, gn_ref, out_ref, sc in ((0, qg_ref, qsb_ref, math.log2(math.e) / math.sqrt(SB_HEAD_DIM)),
                                   (1, kg_ref, ksb_ref, 1.0)):
        u = proj(g)
        out_ref[0] = (u * lax.rsqrt(head_mean_sq(u) + EPS) * (gn_ref[...] * sc)).astype(BF16)
    vsb_ref[0] = proj(2).astype(BF16)
    vml_ref[0] = proj(6).astype(BF16)


def _inproj(x, mod, norm_gain, w_main, wg_t, b_gates, pmat, qg, kg, conv_w, conv_b):
    bsz, seq, d = x.shape
    width = d // 2
    ts = min(IN_TILE, seq)
    ngroups = 9
    const2 = lambda b, s: (0, 0)
    tile_spec = pl.BlockSpec((1, ts, width), lambda b, s: (b, s, 0))
    act = jax.ShapeDtypeStruct((bsz, seq, width), BF16)
    kern = functools.partial(_inproj_kernel, width=width,
                             ml_scale=1.0 / math.sqrt(width // ML_HEADS))
    return pl.pallas_call(
        kern,
        grid=(bsz, seq // ts),
        in_specs=[pl.BlockSpec((1, ts, d), lambda b, s: (b, s, 0)),
                  pl.BlockSpec((3, 1, 1, d), lambda b, s: (0, b, 0, 0)),
                  pl.BlockSpec((1, d), const2),
                  pl.BlockSpec((d, ngroups * width), const2, pipeline_mode=pl.Buffered(1)),
                  pl.BlockSpec((2 * GATE_ROWS, d), const2),
                  pl.BlockSpec((2 * GATE_ROWS, 1), const2),
                  pl.BlockSpec(pmat.shape, const2),
                  pl.BlockSpec((1, width), const2),
                  pl.BlockSpec((1, width), const2),
                  pl.BlockSpec((CONV_WIDTH, 2 * width), const2),
                  pl.BlockSpec((1, 2 * width), const2)],
        out_specs=[tile_spec] * 8 + [pl.BlockSpec((1, 2 * GATE_ROWS, ts), lambda b, s: (b, 0, s))],
        out_shape=[act] * 8 + [jax.ShapeDtypeStruct((bsz, 2 * GATE_ROWS, seq), F32)],
        scratch_shapes=[pltpu.VMEM((2, SUBLANES, width), F32)],
        compiler_params=pltpu.CompilerParams(dimension_semantics=("parallel", "arbitrary"),
                                             vmem_limit_bytes=VMEM_LIMIT),
        name="inproj",
    )(x, mod.reshape(3, bsz, 1, d), norm_gain.reshape(1, d), w_main, wg_t,
      b_gates, pmat, qg, kg, conv_w, conv_b.reshape(1, 2 * width))


def _gate_scan_kernel(gt_ref, scan_ref, o_ref):
    nb, _, seq = gt_ref.shape
    span = scan_ref.shape[1]
    rows = nb * GATE_ROWS

    def per_span(x, mat):
        return jnp.concatenate([_exact_rows_dot(x[:, j:j + span], mat)
                                for j in range(0, seq, span)], axis=1)

    i_pre = gt_ref[:, :GATE_ROWS, :].reshape(rows, seq)
    f_pre = gt_ref[:, GATE_ROWS:, :].reshape(rows, seq)
    bcum = per_span(_log_sigmoid(f_pre), scan_ref[0])
    u = i_pre - bcum
    lane_in_chunk = lax.broadcasted_iota(jnp.int32, (rows, seq), 1) % ML_CHUNK
    u_max = u
    step = 1
    while step < ML_CHUNK:
        shifted = jnp.where(lane_in_chunk >= step, pltpu.roll(u_max, step, 1), -jnp.inf)
        u_max = jnp.maximum(u_max, shifted)
        step *= 2
    ends = per_span(jnp.concatenate([bcum, u_max], axis=0), scan_ref[1])
    groups = (u, bcum, u_max, ends[:rows], ends[rows:])
    for g, val in enumerate(groups):
        o_ref[:, g * GATE_ROWS:(g + 1) * GATE_ROWS, :] = val.reshape(nb, GATE_ROWS, seq)


def _gate_scan(gt):
    bsz, rows, seq = gt.shape
    span = min(GATE_SCAN_SPAN, seq)
    t = jnp.arange(span)
    same_chunk = t[:, None] // ML_CHUNK == t[None, :] // ML_CHUNK
    scan_mats = jnp.stack([same_chunk & (t[:, None] <= t[None, :]),
                           same_chunk & (t[:, None] % ML_CHUNK == ML_CHUNK - 1)]).astype(BF16)
    nb = math.gcd(bsz, GATE_SCAN_BATCH)
    return pl.pallas_call(
        _gate_scan_kernel,
        grid=(bsz // nb,),
        in_specs=[pl.BlockSpec((nb, rows, seq), lambda b: (b, 0, 0)),
                  pl.BlockSpec((2, span, span), lambda b: (0, 0, 0))],
        out_specs=pl.BlockSpec((nb, GATE_GROUPS * GATE_ROWS, seq), lambda b: (b, 0, 0)),
        out_shape=jax.ShapeDtypeStruct((bsz, GATE_GROUPS * GATE_ROWS, seq), F32),
        compiler_params=pltpu.CompilerParams(dimension_semantics=("parallel",),
                                             vmem_limit_bytes=VMEM_LIMIT),
        name="gate_scan",
    )(gt, scan_mats)


def _sb_kernel(q_ref, k_ref, v_ref, g_ref, tri_ref, o_ref):
    seq = q_ref.shape[1]
    tk = SB_TILE
    tq = tk // len(SB_WIDE_PARTS)
    lane_a = lax.broadcasted_iota(jnp.int32, (tq, LANES), 1) < SB_HEAD_DIM
    rows = lax.broadcasted_iota(jnp.int32, (2 * tq, tk), 0)
    cols = lax.broadcasted_iota(jnp.int32, (2 * tq, tk), 1)
    rel = cols - jnp.where(rows < tq, rows, rows - tq)
    ntri = tri_ref[...]

    def softplus(z):
        return jnp.maximum(z, 0.0) + jnp.log2(1.0 + jnp.exp2(-jnp.abs(z)))

    def key_block(z, sp, carry, mask):
        sp_in = sp if mask is None else jnp.where(mask, sp, 0.0)
        r = jnp.dot(sp_in.astype(BF16), ntri, preferred_element_type=F32)
        a = jnp.exp2(z - sp + carry + r[:, :tk])
        if mask is not None:
            a = jnp.where(mask, a, 0.0)
        return a.astype(BF16), carry + r[:, tk:]

    def split_heads(pv):
        return jnp.where(lane_a, pv[:tq], pv[tq:])

    def aligned(x, m):
        return x if isinstance(x, int) else pl.multiple_of(x, m)

    def wide_phases(units):
        idx = range(len(units))
        wide = [w for _, _, w in units]
        st = {}

        def scores():
            st["qab"], st["first"], st["start"], st["z"] = [], [], [], []
            for u, (i, part, _) in enumerate(units):
                q2 = q_ref[0, pl.ds(aligned(i * tk + part * tq, tq), tq), :]
                zero = jnp.zeros_like(q2)
                qab = jnp.concatenate([jnp.where(lane_a, q2, zero), jnp.where(lane_a, zero, q2)],
                                      axis=0)
                first = i - (wide[u] - 1)
                start = aligned(first * tk, tk)
                st["qab"].append(qab)
                st["first"].append(first)
                st["start"].append(start)
                st["z"].append(lax.dot_general(qab, k_ref[0, pl.ds(start, wide[u] * tk), :],
                                               NT_DIMS, preferred_element_type=F32))

        def log_terms():
            st["sp"] = [softplus(z) for z in st["z"]]

        def weights():
            carry = [jnp.zeros((2 * tq, tk), F32) for _ in idx]
            a_parts = [[None] * wide[u] for u in idx]
            for back in range(max(wide)):
                for u, (_, part, _) in enumerate(units):
                    m = wide[u] - 1 - back
                    if m < 0:
                        continue
                    mask = rel < part * tq if back == 0 else None
                    sl = slice(m * tk, (m + 1) * tk)
                    a_parts[u][m], carry[u] = key_block(st["z"][u][:, sl], st["sp"][u][:, sl],
                                                        carry[u], mask)
            st["a"] = [jnp.concatenate(parts, axis=1) for parts in a_parts]
            st["carry"] = carry

        def values():
            st["out"] = [(st["qab"][u], st["first"][u], st["carry"][u],
                          split_heads(jnp.dot(st["a"][u],
                                              v_ref[0, pl.ds(st["start"][u], wide[u] * tk), :],
                                              preferred_element_type=F32)),
                          jnp.max(st["carry"][u])) for u in idx]

        return [scores, log_terms, weights, values], st

    def wide_steps(units):
        if len(units) <= len(SB_WIDE_PARTS):
            phases, st = wide_phases(units)
            for phase in phases:
                phase()
            return st["out"]
        (pa, sta), (pb, stb) = (wide_phases(units[:len(units) // 2]),
                                wide_phases(units[len(units) // 2:]))
        for phase in (pa[0], pa[1], pb[0], pa[2], pb[1], pa[3], pb[2], pb[3]):
            phase()
        return sta["out"] + stb["out"]

    def finish(unit, qab, first, carry, acc, top):
        t0 = aligned(unit[0] * tk + unit[1] * tq, tq)

        def cond(st):
            j, _, _, top = st
            return jnp.logical_and(j >= 0, top > SB_LOG2_STICK_FLOOR)

        def body(st):
            j, carry, acc, _ = st
            s0 = pl.multiple_of(j * tk, tk)
            z = lax.dot_general(qab, k_ref[0, pl.ds(s0, tk), :], NT_DIMS,
                                preferred_element_type=F32)
            a, carry = key_block(z, softplus(z), carry, None)
            pv = jnp.dot(a, v_ref[0, pl.ds(s0, tk), :], preferred_element_type=F32)
            return j - 1, carry, acc + split_heads(pv), jnp.max(carry)

        _, _, acc, _ = lax.while_loop(cond, body, (first - 1, carry, acc, top))
        gate = g_ref[0, pl.ds(t0, tq), :].astype(F32)
        o_ref[0, pl.ds(t0, tq), :] = (acc * gate).astype(BF16)

    def do_units(units):
        for unit, st in zip(units, wide_steps(units)):
            finish(unit, *st)

    def run(lo, hi, count):
        def step(p, _):
            i0 = lo + p * count
            do_units([(i0 + u, part, wide) for u in range(count)
                      for part, wide in enumerate(SB_WIDE_PARTS)])
            return 0
        if hi > lo:
            lax.fori_loop(0, (hi - lo) // count, step, 0)

    nq = seq // tk
    widest = max(SB_WIDE_PARTS)
    n_lead = widest - 1 + (nq - (widest - 1)) % SB_UNROLL
    do_units([(i, part, min(wide, i + 1)) for i in range(widest - 1)
              for part, wide in enumerate(SB_WIDE_PARTS)])
    run(widest - 1, n_lead, 1)
    run(n_lead, nq, SB_UNROLL)


def _sb_attention(q, k, v, g, tri):
    bsz, seq, width = q.shape
    spec = pl.BlockSpec((1, seq, LANES), lambda b, h: (b, 0, h))
    return pl.pallas_call(
        _sb_kernel,
        grid=(bsz, width // LANES),
        in_specs=[spec, spec, spec, spec, pl.BlockSpec(tri.shape, lambda b, h: (0, 0))],
        out_specs=spec,
        out_shape=jax.ShapeDtypeStruct((bsz, seq, width), BF16),
        compiler_params=pltpu.CompilerParams(dimension_semantics=("parallel", "parallel"),
                                             vmem_limit_bytes=VMEM_LIMIT),
        name="sb_attention",
    )(q, k, v, g, tri)


def _ml_kernel(q_ref, k_ref, v_ref, g_ref, gates_ref, gain_ref, sel_ref, o_ref,
               cn_ref, m_ref, x_ref):
    ts = q_ref.shape[1]
    nh = ML_HEADS
    dh = q_ref.shape[2] // nh
    clen = ML_CHUNK
    row = lax.broadcasted_iota(jnp.int32, (clen, clen), 0)
    col = lax.broadcasted_iota(jnp.int32, (clen, clen), 1)
    causal = col <= row
    ones_blk = jnp.ones((clen, dh), BF16)
    mean_mat = jnp.full((dh, dh), 1.0 / dh, BF16)

    @pl.when(pl.program_id(1) == 0)
    def _():
        cn_ref[...] = jnp.zeros(cn_ref.shape, F32)
        m_ref[...] = jnp.zeros(m_ref.shape, F32)

    heads = range(nh)
    lanes = [slice(h * dh, (h + 1) * dh) for h in heads]

    def chunk_phases(c, slot):
        t0 = pl.multiple_of(c * clen, clen)
        st = {}

        def gate_rows(group):
            return gates_ref[0, group * GATE_ROWS:group * GATE_ROWS + nh, pl.ds(t0, clen)]

        def col_tile(vec, h):
            j = (vec * nh + h) * LANES
            return st["bcast"][:, j:j + LANES]

        def scores_and_gates():
            st["q"] = [q_ref[0, pl.ds(t0, clen), lanes[h]] for h in heads]
            st["k"] = [k_ref[0, pl.ds(t0, clen), lanes[h]] for h in heads]
            st["qk"] = [lax.dot_general(st["q"][h], st["k"][h], NT_DIMS,
                                        preferred_element_type=F32) for h in heads]
            u, bcum, u_max, b_last, u_top = (gate_rows(g) for g in range(GATE_GROUPS))
            m_prev = m_ref[...]
            m_rel = jnp.maximum(m_prev, u_max)
            m_new = b_last + jnp.maximum(m_prev, u_top)
            m_ref[...] = m_new
            st["u"] = u
            st["g_c"] = jnp.exp(b_last + m_prev - m_new)
            x_ref[slot, 0 * nh:1 * nh] = -m_rel
            x_ref[slot, 1 * nh:2 * nh] = jnp.exp(m_prev - m_rel)
            x_ref[slot, 2 * nh:3 * nh] = jnp.exp(jnp.minimum(-(bcum + m_rel), ML_MAX_EXP_ARG))
            x_ref[slot, 3 * nh:4 * nh] = jnp.exp(b_last + u - m_new)
            st["bcast"] = lax.dot_general(_split3(x_ref[slot]), sel_ref[...], TN_DIMS,
                                          preferred_element_type=F32)

        def outputs():
            st["cn"] = [cn_ref[h] for h in heads]
            inter = [jnp.dot(st["q"][h], st["cn"][h].astype(BF16), preferred_element_type=F32)
                     for h in heads]
            st["v_aug"] = [jnp.concatenate([v_ref[0, pl.ds(t0, clen), lanes[h]], ones_blk], axis=1)
                           for h in heads]
            sc = [st["qk"][h] * jnp.exp(jnp.where(causal, st["u"][h:h + 1, :] + col_tile(0, h),
                                                  -jnp.inf)) for h in heads]
            intra = [jnp.dot(sc[h].astype(BF16), st["v_aug"][h], preferred_element_type=F32)
                     for h in heads]
            st["h_out"] = []
            for h in heads:
                g_inter = col_tile(1, h)
                num = intra[h][:, :dh] + g_inter * inter[h][:, :dh]
                den = intra[h][:, dh:] + g_inter * inter[h][:, dh:]
                st["h_out"].append(num / jnp.maximum(jnp.abs(den), col_tile(2, h)))
            st["ms"] = [jnp.dot((st["h_out"][h] * st["h_out"][h]).astype(BF16), mean_mat,
                                preferred_element_type=F32) for h in heads]

        def state_update():
            kw = [(st["k"][h].astype(F32) * col_tile(3, h)).astype(BF16) for h in heads]
            upd = [lax.dot_general(kw[h], st["v_aug"][h], TN_DIMS, preferred_element_type=F32)
                   for h in heads]
            for h in heads:
                decay = jnp.concatenate([st["g_c"][h:h + 1, :]] * (2 * dh // clen), axis=1)
                cn_ref[h] = decay * st["cn"][h] + upd[h]

        def store():
            for h in heads:
                hn = st["h_out"][h] * lax.rsqrt(st["ms"][h] + EPS) * gain_ref[:, lanes[h]]
                gate = g_ref[0, pl.ds(t0, clen), lanes[h]].astype(F32)
                o_ref[0, pl.ds(t0, clen), lanes[h]] = (hn * gate).astype(BF16)

        return scores_and_gates, outputs, state_update, store

    def chunk_pair(p, _):
        a = chunk_phases(2 * p, 0)
        b = chunk_phases(2 * p + 1, 1)
        for phase in (a[0], b[0], a[1], a[2], b[1], a[3], b[2], b[3]):
            phase()
        return 0

    lax.fori_loop(0, ts // (2 * clen), chunk_pair, 0)


def _mlstm(q, k, v, g, gates, gain):
    bsz, seq, width = q.shape
    dh = width // ML_HEADS
    ts = min(ML_TILE, seq)
    nvec = 4 * ML_HEADS
    sel = (jnp.arange(4 * nvec)[:, None] % nvec == jnp.arange(nvec * LANES)[None, :] // LANES)
    spec = pl.BlockSpec((1, ts, width), lambda b, s: (b, s, 0))
    return pl.pallas_call(
        _ml_kernel,
        grid=(bsz, seq // ts),
        in_specs=[spec, spec, spec, spec,
                  pl.BlockSpec((1, GATE_GROUPS * GATE_ROWS, ts), lambda b, s: (b, 0, s)),
                  pl.BlockSpec((1, width), lambda b, s: (0, 0)),
                  pl.BlockSpec((4 * nvec, nvec * LANES), lambda b, s: (0, 0))],
        out_specs=spec,
        out_shape=jax.ShapeDtypeStruct((bsz, seq, width), BF16),
        scratch_shapes=[pltpu.VMEM((ML_HEADS, dh, 2 * dh), F32),
                        pltpu.VMEM((ML_HEADS, ML_CHUNK), F32),
                        pltpu.VMEM((2, nvec, ML_CHUNK), F32)],
        compiler_params=pltpu.CompilerParams(dimension_semantics=("parallel", "arbitrary"),
                                             vmem_limit_bytes=VMEM_LIMIT),
        name="mlstm",
    )(q, k, v, g, gates, gain.reshape(1, width), sel.astype(BF16))


def _outproj_kernel(ysb_ref, yml_ref, x_ref, mod_ref, w_ref, o_ref, *, width):
    y = (jnp.dot(ysb_ref[0], w_ref[:width, :], preferred_element_type=F32)
         + jnp.dot(yml_ref[0], w_ref[width:, :], preferred_element_type=F32))
    o_ref[0] = x_ref[0] + mod_ref[2, 0] * y


def _outproj(ysb, yml, x, mod, w_out):
    bsz, seq, d = x.shape
    width = ysb.shape[2]
    ts = min(OUT_TILE, seq)
    ytile = pl.BlockSpec((1, ts, width), lambda b, s: (b, s, 0))
    xtile = pl.BlockSpec((1, ts, d), lambda b, s: (b, s, 0))
    return pl.pallas_call(
        functools.partial(_outproj_kernel, width=width),
        grid=(bsz, seq // ts),
        in_specs=[ytile, ytile, xtile,
                  pl.BlockSpec((3, 1, 1, d), lambda b, s: (0, b, 0, 0)),
                  pl.BlockSpec(w_out.shape, lambda b, s: (0, 0))],
        out_specs=xtile,
        out_shape=jax.ShapeDtypeStruct(x.shape, x.dtype),
        compiler_params=pltpu.CompilerParams(dimension_semantics=("parallel", "parallel"),
                                             vmem_limit_bytes=VMEM_LIMIT),
        name="outproj",
    )(ysb, yml, x, mod.reshape(3, bsz, 1, d), w_out)


def _layer(x, c, w_ada, b_ada, norm_gain, w_in, b_gates, q_norm_gain, k_norm_gain,
           conv_w, conv_b, ml_norm_gain, w_out):
    bsz, seq, d = x.shape
    width = d // 2
    assert d % (2 * LANES) == 0 and width == ML_HEADS * LANES
    assert seq % IN_TILE == 0 and seq % SB_TILE == 0 and seq % ML_TILE == 0
    assert ML_TILE % (2 * ML_CHUNK) == 0 and ML_CHUNK == LANES
    assert seq >= max(SB_WIDE_PARTS) * SB_TILE and SB_TILE % len(SB_WIDE_PARTS) == 0
    nmain = 9 * width
    w_main = w_in.astype(BF16)
    pad = ((0, GATE_ROWS - ML_HEADS), (0, 0))
    wg = w_in[:, nmain:].T
    wg_t = jnp.concatenate([jnp.pad(wg[:ML_HEADS], pad), jnp.pad(wg[ML_HEADS:], pad)]).astype(BF16)
    bg = b_gates.reshape(2 * ML_HEADS, 1)
    bg = jnp.concatenate([jnp.pad(bg[:ML_HEADS], pad), jnp.pad(bg[ML_HEADS:], pad)])
    heads = width // SB_HEAD_DIM
    head_id = jnp.arange(min(MXU_DIM, width)) // SB_HEAD_DIM
    pmat = jnp.where(head_id[:, None] == head_id[None, :], 1.0 / SB_HEAD_DIM, 0.0).astype(BF16)
    qg = jnp.tile(q_norm_gain, heads).reshape(1, width)
    kg = jnp.tile(k_norm_gain, heads).reshape(1, width)
    kk = jnp.arange(SB_TILE)
    tri = -jnp.concatenate([(kk[:, None] > kk[None, :]).astype(BF16),
                            jnp.ones((SB_TILE, SB_TILE), BF16)], axis=1)

    mod = _mod(c, w_ada, b_ada)
    (qsb, ksb, vsb, gsb, qml, kml, vml, gml, gates) = _inproj(
        x, mod, norm_gain, w_main, wg_t, bg, pmat, qg, kg, conv_w, conv_b)
    ysb = _sb_attention(qsb, ksb, vsb, gsb, tri)
    yml = _mlstm(qml, kml, vml, gml, _gate_scan(gates), ml_norm_gain)
    return _outproj(ysb, yml, x, mod, w_out.astype(BF16))


def kernel(x, c, w_ada, b_ada, norm_gain, w_in, b_gates, q_norm_gain, k_norm_gain, conv_w,
           conv_b, ml_norm_gain, w_out):
    h = x
    for layer in range(w_in.shape[0]):
        h = _layer(h, c, w_ada[layer], b_ada[layer], norm_gain[layer], w_in[layer],
                   b_gates[layer], q_norm_gain[layer], k_norm_gain[layer], conv_w[layer],
                   conv_b[layer], ml_norm_gain[layer], w_out[layer])
    return h
```
